```python
import jax, jax.numpy as jnp
from jax import lax
import numpy as np

D_MODEL = 2048
BATCH = 8
SEQ = 2048
DEPTH = 1
DEC_BATCH = 8
DEC_SEQ = 16
PAST_LEN = 2048

CHUNK = 64
CONV_DIM = 1024
CONV_WIDTH = 31
DN_HEADS = 8
DN_DK = 128
DN_DV = 128
DN_KEY = DN_HEADS * DN_DK
DN_VAL = DN_HEADS * DN_DV
DN_CONV_WIDTH = 4
DN_CONV_CH = 2 * DN_KEY + DN_VAL
N_BRANCH = 2
PEER_HEADS = 8
PEER_QDIM = 256
PEER_HALF = PEER_QDIM // 2
N_KEYS = 128
N_EXPERTS = N_KEYS * N_KEYS
PEER_TOPK = 16
PEER_BLOCK = 64
EPS = 1e-6
IN_SPLITS = (2 * CONV_DIM, DN_CONV_CH, DN_HEADS, DN_HEADS, DN_VAL, N_BRANCH * D_MODEL)
IN_DIM = sum(IN_SPLITS)

kernel_name = 'streaming_conv_gdn_peer'


def rms_norm(x, g):
    xf = x.astype(jnp.float32)
    y = xf * lax.rsqrt(jnp.mean(xf * xf, axis=-1, keepdims=True) + EPS)
    return (y * g.astype(jnp.float32)).astype(x.dtype)


def layer_norm(x, g, b):
    xf = x.astype(jnp.float32)
    mu = jnp.mean(xf, axis=-1, keepdims=True)
    var = jnp.mean(jnp.square(xf - mu), axis=-1, keepdims=True)
    y = (xf - mu) * lax.rsqrt(var + EPS)
    return (y * g.astype(jnp.float32) + b.astype(jnp.float32)).astype(x.dtype)


def l2_normalize(x):
    return x * lax.rsqrt(jnp.sum(x * x, axis=-1, keepdims=True) + EPS)


def causal_dwconv(x, prefix, w):
    xp = jnp.concatenate([prefix.astype(x.dtype), x], axis=1)
    y = lax.conv_general_dilated(xp, w[:, None, :].astype(x.dtype), window_strides=(1,), padding='VALID',
                                 dimension_numbers=('NWC', 'WIO', 'NWC'), feature_group_count=x.shape[-1])
    return y, xp[:, -(w.shape[0] - 1):]


def gated_delta_chunked(q, k, v, beta, g, s0, chunk):
    bsz, t_len, h, dk = q.shape
    dv = v.shape[-1]
    n_blk = t_len // chunk

    def blk(a):
        return jnp.swapaxes(a.reshape((bsz, n_blk, chunk) + a.shape[2:]), 2, 3)

    q, k, v, beta, g = blk(q), blk(k), blk(v), blk(beta), blk(g)
    G = jnp.cumsum(g, axis=-1)
    causal = jnp.tril(jnp.ones((chunk, chunk), bool))
    strict = jnp.tril(jnp.ones((chunk, chunk), bool), -1)
    decay = jnp.exp(jnp.where(causal, G[..., :, None] - G[..., None, :], -jnp.inf))
    kk = jnp.einsum('bnhid,bnhjd->bnhij', k, k)
    a_mat = jnp.where(strict, beta[..., :, None] * kk * decay, 0.0) + jnp.eye(chunk, dtype=jnp.float32)
    gam = jnp.exp(G)
    rhs = jnp.concatenate([beta[..., None] * v, (beta * gam)[..., None] * k], axis=-1)
    sol = lax.linalg.triangular_solve(a_mat, rhs, left_side=True, lower=True)
    u_blk, w_blk = sol[..., :dv], sol[..., dv:]
    qk = jnp.einsum('bnhid,bnhjd->bnhij', q, k) * decay
    qg = q * gam[..., None]
    kd = k * jnp.exp(G[..., -1:] - G)[..., None]
    gl = jnp.exp(G[..., -1])

    def step(s, inp):
        u_c, w_c, qk_c, qg_c, kd_c, gl_c = inp
        delta = u_c - jnp.einsum('bhck,bhkv->bhcv', w_c, s)
        o = jnp.einsum('bhck,bhkv->bhcv', qg_c, s) + jnp.einsum('bhij,bhjv->bhiv', qk_c, delta)
        s = gl_c[..., None, None] * s + jnp.einsum('bhck,bhcv->bhkv', kd_c, delta)
        return s, o

    xs = (jnp.moveaxis(u_blk, 1, 0), jnp.moveaxis(w_blk, 1, 0), jnp.moveaxis(qk, 1, 0),
          jnp.moveaxis(qg, 1, 0), jnp.moveaxis(kd, 1, 0), jnp.moveaxis(gl, 1, 0))
    s_fin, o = lax.scan(step, s0, xs)
    o = jnp.swapaxes(jnp.moveaxis(o, 0, 1), 2, 3).reshape(bsz, t_len, h, dv)
    return o, s_fin


def mixer(xn, conv_prev, dconv_prev, s_prev, w_in, conv_dw_w, conv_dw_b, conv_ln_g, conv_ln_b, conv_pw2,
          dn_conv_w, dn_a_log, dn_dt_bias, dn_norm_g, dn_w_o, w_out, chunk):
    bsz, t_len, _ = xn.shape
    f32 = jnp.float32
    pts = [sum(IN_SPLITS[:i + 1]) for i in range(len(IN_SPLITS) - 1)]
    glu, qkv, b_raw, a_raw, z, gates = jnp.split(xn @ w_in, pts, axis=-1)
    c_in = glu[..., :CONV_DIM] * jax.nn.sigmoid(glu[..., CONV_DIM:])
    c, conv_new = causal_dwconv(c_in, conv_prev, conv_dw_w)
    c = jax.nn.silu(layer_norm(c + conv_dw_b, conv_ln_g, conv_ln_b))
    y_conv = c @ conv_pw2
    qkv, dconv_new = causal_dwconv(qkv, dconv_prev, dn_conv_w)
    qkv = jax.nn.silu(qkv)
    q, k, v = jnp.split(qkv, [DN_KEY, 2 * DN_KEY], axis=-1)
    q = l2_normalize(q.reshape(bsz, t_len, DN_HEADS, DN_DK).astype(f32)) * (DN_DK ** -0.5)
    k = l2_normalize(k.reshape(bsz, t_len, DN_HEADS, DN_DK).astype(f32))
    v = v.reshape(bsz, t_len, DN_HEADS, DN_DV).astype(f32)
    beta = jax.nn.sigmoid(b_raw.astype(f32))
    g = -jnp.exp(dn_a_log.astype(f32)) * jax.nn.softplus(a_raw.astype(f32) + dn_dt_bias.astype(f32))
    o, s_new = gated_delta_chunked(q, k, v, beta, g, s_prev.astype(f32), chunk)
    o = o * lax.rsqrt(jnp.mean(o * o, axis=-1, keepdims=True) + EPS) * dn_norm_g.astype(f32)
    o = o * jax.nn.silu(z.reshape(bsz, t_len, DN_HEADS, DN_DV).astype(f32))
    y_dn = o.reshape(bsz, t_len, DN_VAL).astype(xn.dtype) @ dn_w_o
    g_conv, g_dn = jnp.split(jax.nn.sigmoid(gates), 2, axis=-1)
    out = (g_conv * y_conv + g_dn * y_dn) @ w_out
    return out, conv_new, dconv_new, s_new.astype(xn.dtype)


def peer(xn, w_q, sub_keys, u, v):
    bsz, t_len, d = xn.shape
    x2 = xn.reshape(-1, d)
    n = x2.shape[0]
    q = (x2 @ w_q).reshape(n, PEER_HEADS, 2, PEER_HALF)
    s = jnp.einsum('nhpd,hpkd->nhpk', q, sub_keys).astype(jnp.float32)
    s_top, i_top = lax.top_k(s, PEER_TOPK)
    cand = (s_top[:, :, 0, :, None] + s_top[:, :, 1, None, :]).reshape(n, PEER_HEADS, PEER_TOPK * PEER_TOPK)
    cand_idx = (i_top[:, :, 0, :, None] * N_KEYS + i_top[:, :, 1, None, :]).reshape(n, PEER_HEADS, PEER_TOPK * PEER_TOPK)
    s_fin, pos = lax.top_k(cand, PEER_TOPK)
    idx = jnp.take_along_axis(cand_idx, pos, axis=-1)
    gate = jax.nn.softmax(s_fin, axis=-1).astype(xn.dtype)
    n_blk = -(-n // PEER_BLOCK)
    pad = n_blk * PEER_BLOCK - n
    xp = jnp.pad(x2, ((0, pad), (0, 0))).reshape(n_blk, PEER_BLOCK, d)
    ip = jnp.pad(idx, ((0, pad), (0, 0), (0, 0))).reshape(n_blk, PEER_BLOCK, PEER_HEADS, PEER_TOPK)
    gp = jnp.pad(gate, ((0, pad), (0, 0), (0, 0))).reshape(n_blk, PEER_BLOCK, PEER_HEADS, PEER_TOPK)

    def block(args):
        xb, ib, gb = args
        hid = jnp.einsum('pd,phkd->phk', xb, u[ib])
        act = jax.nn.gelu(hid) * gb
        return jnp.einsum('phk,phkd->pd', act, v[ib])

    out = lax.map(block, (xp, ip, gp)).reshape(-1, d)[:n]
    return out.reshape(bsz, t_len, d)


def trunk(x, conv_st, dconv_st, dn_st, W, norm_final_g, chunk):
    new_c, new_dc, new_s = [], [], []
    for l in range(DEPTH):
        xn = rms_norm(x, W['norm_mix_g'][l])
        out, c_st, dc_st, s_st = mixer(xn, conv_st[l], dconv_st[l], dn_st[l], W['w_in'][l], W['conv_dw_w'][l],
                                       W['conv_dw_b'][l], W['conv_ln_g'][l], W['conv_ln_b'][l], W['conv_pw2'][l],
                                       W['dn_conv_w'][l], W['dn_a_log'][l], W['dn_dt_bias'][l], W['dn_norm_g'][l],
                                       W['dn_w_o'][l], W['w_out'][l], chunk)
        x = x + out
        x = x + peer(rms_norm(x, W['norm_ffn_g'][l]), W['peer_w_q'][l], W['peer_sub_keys'][l],
                     W['peer_u'][l], W['peer_v'][l])
        new_c.append(c_st)
        new_dc.append(dc_st)
        new_s.append(s_st)
    y = rms_norm(x, norm_final_g)
    return y, jnp.stack(new_c), jnp.stack(new_dc), jnp.stack(new_s)


def setup_inputs(seed: int = 0) -> dict:
    key = jax.random.key(seed)
    ks = jax.random.split(key, 32)
    f32 = jnp.float32

    def nrm(k, shape, s):
        return jax.random.normal(k, shape, f32) * s

    L = DEPTH
    dt = jnp.exp(jax.random.uniform(ks[13], (L, DN_HEADS), f32, np.log(1e-3), np.log(1e-1)))
    return {
        'x_prompt': nrm(ks[0], (BATCH, SEQ, D_MODEL), 1.0),
        'x_sample': nrm(ks[1], (DEC_BATCH, DEC_SEQ, D_MODEL), 1.0),
        'state_conformer_conv': nrm(ks[2], (L, DEC_BATCH, CONV_WIDTH - 1, CONV_DIM), 0.5),
        'state_delta_conv': nrm(ks[3], (L, DEC_BATCH, DN_CONV_WIDTH - 1, DN_CONV_CH), 1.0),
        'state_delta': nrm(ks[4], (L, DEC_BATCH, DN_HEADS, DN_DK, DN_DV), 0.1),
        'norm_mix_g': 1.0 + nrm(ks[5], (L, D_MODEL), 0.02),
        'w_in': nrm(ks[6], (L, D_MODEL, IN_DIM), D_MODEL ** -0.5),
        'conv_dw_w': nrm(ks[7], (L, CONV_WIDTH, CONV_DIM), CONV_WIDTH ** -0.5),
        'conv_dw_b': nrm(ks[8], (L, CONV_DIM), 0.02),
        'conv_ln_g': 1.0 + nrm(ks[9], (L, CONV_DIM), 0.02),
        'conv_ln_b': nrm(ks[10], (L, CONV_DIM), 0.02),
        'conv_pw2': nrm(ks[11], (L, CONV_DIM, D_MODEL), CONV_DIM ** -0.5),
        'dn_conv_w': nrm(ks[12], (L, DN_CONV_WIDTH, DN_CONV_CH), DN_CONV_WIDTH ** -0.5),
        'dn_a_log': jnp.log(jax.random.uniform(ks[14], (L, DN_HEADS), f32, 1.0, 16.0)),
        'dn_dt_bias': dt + jnp.log(-jnp.expm1(-dt)),
        'dn_norm_g': 1.0 + nrm(ks[15], (L, DN_DV), 0.02),
        'dn_w_o': nrm(ks[16], (L, DN_VAL, D_MODEL), DN_VAL ** -0.5),
        'w_out': nrm(ks[17], (L, D_MODEL, D_MODEL), D_MODEL ** -0.5),
        'norm_ffn_g': 1.0 + nrm(ks[18], (L, D_MODEL), 0.02),
        'peer_w_q': nrm(ks[19], (L, D_MODEL, PEER_HEADS * PEER_QDIM), D_MODEL ** -0.5),
        'peer_sub_keys': nrm(ks[20], (L, PEER_HEADS, 2, N_KEYS, PEER_HALF), PEER_HALF ** -0.5),
        'peer_u': nrm(ks[21], (L, N_EXPERTS, D_MODEL), D_MODEL ** -0.5),
        'peer_v': nrm(ks[22], (L, N_EXPERTS, D_MODEL), 0.5),
        'norm_final_g': 1.0 + nrm(ks[23], (D_MODEL,), 0.02),
    }


def reference(x_prompt, x_sample, state_conformer_conv, state_delta_conv, state_delta, norm_mix_g, w_in,
              conv_dw_w, conv_dw_b, conv_ln_g, conv_ln_b, conv_pw2, dn_conv_w, dn_a_log, dn_dt_bias, dn_norm_g,
              dn_w_o, w_out, norm_ffn_g, peer_w_q, peer_sub_keys, peer_u, peer_v, norm_final_g):
    W = {'norm_mix_g': norm_mix_g, 'w_in': w_in, 'conv_dw_w': conv_dw_w, 'conv_dw_b': conv_dw_b,
         'conv_ln_g': conv_ln_g, 'conv_ln_b': conv_ln_b, 'conv_pw2': conv_pw2, 'dn_conv_w': dn_conv_w,
         'dn_a_log': dn_a_log, 'dn_dt_bias': dn_dt_bias, 'dn_norm_g': dn_norm_g, 'dn_w_o': dn_w_o,
         'w_out': w_out, 'norm_ffn_g': norm_ffn_g, 'peer_w_q': peer_w_q, 'peer_sub_keys': peer_sub_keys,
         'peer_u': peer_u, 'peer_v': peer_v}
    bp = x_prompt.shape[0]
    dt_ = x_prompt.dtype
    zc = jnp.zeros((DEPTH, bp, CONV_WIDTH - 1, CONV_DIM), dt_)
    zdc = jnp.zeros((DEPTH, bp, DN_CONV_WIDTH - 1, DN_CONV_CH), dt_)
    zs = jnp.zeros((DEPTH, bp, DN_HEADS, DN_DK, DN_DV), dt_)
    y_prompt, pc, pdc, ps = trunk(x_prompt, zc, zdc, zs, W, norm_final_g, CHUNK)
    y_sample, sc, sdc, ss = trunk(x_sample, state_conformer_conv, state_delta_conv, state_delta, W,
                                  norm_final_g, x_sample.shape[1])
    return (y_prompt, y_sample, pc, pdc, ps, sc, sdc, ss)
```

```python
import functools
import math

import jax
import jax.numpy as jnp
from jax import lax
from jax.experimental import pallas as pl
from jax.experimental.pallas import tpu as pltpu

F32 = jnp.float32
BF16 = jnp.bfloat16
HI = lax.Precision.HIGHEST

EPS = 1e-6
D_MODEL = 2048
CONV_DIM = 1024
CONV_WIDTH = 31
DN_HEADS = 8
DN_DK = 128
DN_DV = 128
DN_KEY = DN_HEADS * DN_DK
DN_VAL = DN_HEADS * DN_DV
DN_CONV_WIDTH = 4
DN_CONV_CH = 2 * DN_KEY + DN_VAL
PEER_HEADS = 8
PEER_HALF = 128
N_KEYS = 128
N_EXPERTS = N_KEYS * N_KEYS
PEER_TOPK = 16

LANES = 128
SUBLANES = 8
VMEM_LIMIT = 56 * 1024 * 1024

COL_QKV = 0
COL_GLU = COL_QKV + DN_CONV_CH
COL_Z = COL_GLU + 2 * CONV_DIM
COL_GATE = COL_Z + DN_VAL
COL_BA = COL_GATE + 2 * D_MODEL
PROJ_COLS = COL_BA + LANES


def _params(*sem):
    return pltpu.CompilerParams(dimension_semantics=sem, vmem_limit_bytes=VMEM_LIMIT)


def _const_spec(shape):
    nd = len(shape)
    return pl.BlockSpec(shape, lambda *_: (0,) * nd, pipeline_mode=pl.Buffered(1))


def _sigmoid(x):
    return 1.0 / (1.0 + jnp.exp(-x))


def _silu(x):
    return x * _sigmoid(x)


def _in_proj_kernel(x_ref, g_ref, w_ref, o_ref, xn_ref, *, rows):
    @pl.when(pl.program_id(1) == 0)
    def _():
        def body(i, carry):
            r = pl.ds(pl.multiple_of(i * rows, rows), rows)
            x = x_ref[r, :]
            y = x * lax.rsqrt(jnp.mean(x * x, axis=-1, keepdims=True) + EPS)
            xn_ref[r, :] = (y * g_ref[...]).astype(BF16)
            return carry
        lax.fori_loop(0, x_ref.shape[0] // rows, body, 0)

    o_ref[...] = jnp.dot(xn_ref[...], w_ref[...], preferred_element_type=F32)


def _in_proj(x2, g, w_pack):
    n = x2.shape[0]
    tm = min(n, 1024)
    tn = 1152
    return pl.pallas_call(
        functools.partial(_in_proj_kernel, rows=min(tm, 64)),
        grid=(n // tm, PROJ_COLS // tn),
        in_specs=[pl.BlockSpec((tm, D_MODEL), lambda i, j: (i, 0)),
                  pl.BlockSpec((1, D_MODEL), lambda i, j: (0, 0)),
                  pl.BlockSpec((D_MODEL, tn), lambda i, j: (0, j))],
        out_specs=pl.BlockSpec((tm, tn), lambda i, j: (i, j)),
        out_shape=jax.ShapeDtypeStruct((n, PROJ_COLS), F32),
        scratch_shapes=[pltpu.VMEM((tm, D_MODEL), BF16)],
        compiler_params=_params("arbitrary", "arbitrary"),
        name="in_proj",
    )(x2, g, w_pack)


def _conv_rows(xp_ref, w_ref, r0, rows, lanes, first, kw):
    acc = None
    k = 0
    while k < kw:
        off = first + k
        base = (off // SUBLANES) * SUBLANES
        nk = min(kw - k, base + SUBLANES - off)
        extra = SUBLANES if off + nk - 1 > base else 0
        win = xp_ref[pl.ds(pl.multiple_of(r0 + base, SUBLANES), rows + extra), lanes]
        for kk in range(k, k + nk):
            s = first + kk - base
            term = w_ref[kk:kk + 1, lanes] * win[s:s + rows, :]
            acc = term if acc is None else acc + term
        k += nk
    return acc


CONV_HDR = 32


def _conv_branch_kernel(a_ref, b_ref, prev_ref, w_ref, bias_ref, lng_ref, lnb_ref,
                        c_ref, new_ref, xp_ref, *, tt, rows):
    t = pl.program_id(1)
    p = CONV_WIDTH - 1

    @pl.when(t == 0)
    def _():
        xp_ref[CONV_HDR - p:CONV_HDR, :] = prev_ref[0]

    @pl.when(t > 0)
    def _():
        xp_ref[CONV_HDR - p:CONV_HDR, :] = xp_ref[CONV_HDR + tt - p:CONV_HDR + tt, :]

    xp_ref[CONV_HDR:CONV_HDR + tt, :] = a_ref[...] * _sigmoid(b_ref[...])

    def body(i, carry):
        r0 = pl.multiple_of(i * rows, rows)
        y = _conv_rows(xp_ref, w_ref, r0, rows, slice(None), CONV_HDR - p, CONV_WIDTH) + bias_ref[...]
        mu = jnp.mean(y, axis=-1, keepdims=True)
        yc = y - mu
        var = jnp.mean(yc * yc, axis=-1, keepdims=True)
        z = yc * lax.rsqrt(var + EPS) * lng_ref[...] + lnb_ref[...]
        c_ref[pl.ds(r0, rows), :] = _silu(z).astype(BF16)
        return carry
    lax.fori_loop(0, tt // rows, body, 0)

    new_ref[0] = xp_ref[CONV_HDR + tt - p:CONV_HDR + tt, :]


def _conv_branch(proj, conv_prev, w, bias, lng, lnb, bsz, t_len):
    tt = min(t_len, 256)
    nt = t_len // tt
    p = CONV_WIDTH - 1
    blk_a = COL_GLU // CONV_DIM
    return pl.pallas_call(
        functools.partial(_conv_branch_kernel, tt=tt, rows=16),
        grid=(bsz, nt),
        in_specs=[pl.BlockSpec((tt, CONV_DIM), lambda b, t: (b * nt + t, blk_a)),
                  pl.BlockSpec((tt, CONV_DIM), lambda b, t: (b * nt + t, blk_a + 1)),
                  pl.BlockSpec((1, p, CONV_DIM), lambda b, t: (b, 0, 0)),
                  pl.BlockSpec((CONV_WIDTH, CONV_DIM), lambda b, t: (0, 0)),
                  pl.BlockSpec((1, CONV_DIM), lambda b, t: (0, 0)),
                  pl.BlockSpec((1, CONV_DIM), lambda b, t: (0, 0)),
                  pl.BlockSpec((1, CONV_DIM), lambda b, t: (0, 0))],
        out_specs=[pl.BlockSpec((tt, CONV_DIM), lambda b, t: (b * nt + t, 0)),
                   pl.BlockSpec((1, p, CONV_DIM), lambda b, t: (b, 0, 0))],
        out_shape=[jax.ShapeDtypeStruct((bsz * t_len, CONV_DIM), BF16),
                   jax.ShapeDtypeStruct((bsz, p, CONV_DIM), F32)],
        scratch_shapes=[pltpu.VMEM((CONV_HDR + tt, CONV_DIM), F32)],
        compiler_params=_params("arbitrary", "arbitrary"),
        name="conv_branch",
    )(proj, proj, conv_prev, w, bias, lng, lnb)


DN_HDR = 8


def _dn_prep_kernel(x_ref, ba_ref, prev_ref, w_ref, alog_ref, dt_ref,
                    qkv_ref, bg_ref, new_ref, xp_ref, *, tt, rows):
    t = pl.program_id(1)
    p = DN_CONV_WIDTH - 1

    @pl.when(t == 0)
    def _():
        xp_ref[DN_HDR - p:DN_HDR, :] = prev_ref[0]

    @pl.when(t > 0)
    def _():
        xp_ref[DN_HDR - p:DN_HDR, :] = xp_ref[DN_HDR + tt - p:DN_HDR + tt, :]

    xp_ref[DN_HDR:DN_HDR + tt, :] = x_ref[...]

    def body(i, carry):
        r0 = pl.multiple_of(i * rows, rows)
        for hs in range(DN_CONV_CH // LANES):
            lanes = slice(hs * LANES, (hs + 1) * LANES)
            y = _silu(_conv_rows(xp_ref, w_ref, r0, rows, lanes, DN_HDR - p, DN_CONV_WIDTH))
            if hs < 2 * DN_HEADS:
                y = y * lax.rsqrt(jnp.sum(y * y, axis=-1, keepdims=True) + EPS)
                if hs < DN_HEADS:
                    y = y * (DN_DK ** -0.5)
            qkv_ref[pl.ds(r0, rows), lanes] = y
        return carry
    lax.fori_loop(0, tt // rows, body, 0)

    raw = ba_ref[...]
    sp_in = raw + dt_ref[...]
    softplus = jnp.maximum(sp_in, 0.0) + jnp.log(1.0 + jnp.exp(-jnp.abs(sp_in)))
    lane = lax.broadcasted_iota(jnp.int32, raw.shape, 1)
    bg_ref[...] = jnp.where(lane < DN_HEADS, _sigmoid(raw), -jnp.exp(alog_ref[...]) * softplus)

    new_ref[0] = xp_ref[DN_HDR + tt - p:DN_HDR + tt, :]


def _dn_prep(proj, dconv_prev, w, alog_pad, dt_pad, bsz, t_len):
    tt = min(t_len, 256)
    nt = t_len // tt
    p = DN_CONV_WIDTH - 1
    return pl.pallas_call(
        functools.partial(_dn_prep_kernel, tt=tt, rows=16),
        grid=(bsz, nt),
        in_specs=[pl.BlockSpec((tt, DN_CONV_CH), lambda b, t: (b * nt + t, COL_QKV // DN_CONV_CH)),
                  pl.BlockSpec((tt, LANES), lambda b, t: (b * nt + t, COL_BA // LANES)),
                  pl.BlockSpec((1, p, DN_CONV_CH), lambda b, t: (b, 0, 0)),
                  pl.BlockSpec((DN_CONV_WIDTH, DN_CONV_CH), lambda b, t: (0, 0)),
                  pl.BlockSpec((1, LANES), lambda b, t: (0, 0)),
                  pl.BlockSpec((1, LANES), lambda b, t: (0, 0))],
        out_specs=[pl.BlockSpec((tt, DN_CONV_CH), lambda b, t: (b * nt + t, 0)),
                   pl.BlockSpec((tt, LANES), lambda b, t: (b * nt + t, 0)),
                   pl.BlockSpec((1, p, DN_CONV_CH), lambda b, t: (b, 0, 0))],
        out_shape=[jax.ShapeDtypeStruct((bsz * t_len, DN_CONV_CH), F32),
                   jax.ShapeDtypeStruct((bsz * t_len, LANES), F32),
                   jax.ShapeDtypeStruct((bsz, p, DN_CONV_CH), F32)],
        scratch_shapes=[pltpu.VMEM((DN_HDR + tt, DN_CONV_CH), F32)],
        compiler_params=_params("arbitrary", "arbitrary"),
        name="dn_prep",
    )(proj, proj, dconv_prev, w, alog_pad, dt_pad)


def _dot_hi(a, b):
    return jnp.dot(a, b, precision=HI, preferred_element_type=F32)


def _dot_nt_hi(a, b):
    return lax.dot_general(a, b, (((1,), (1,)), ((), ())), precision=HI, preferred_element_type=F32)


def _dot_tn_hi(a, b):
    return lax.dot_general(a, b, (((0,), (0,)), ((), ())), precision=HI, preferred_element_type=F32)


def _gated_delta_kernel(q_ref, k_ref, v_ref, bcol_ref, gcol_ref, grow_ref, s0_ref, o_ref, s_ref, *, chunk):
    c = pl.program_id(1)

    @pl.when(c == 0)
    def _():
        s_ref[...] = s0_ref[...]

    row = lax.broadcasted_iota(jnp.int32, (chunk, chunk), 0)
    col = lax.broadcasted_iota(jnp.int32, (chunk, chunk), 1)
    causal = row >= col
    strict = row > col
    lower = causal.astype(F32)
    upper = (row <= col).astype(F32)
    eye = (row == col).astype(F32)

    g_cum_col = _dot_hi(lower, gcol_ref[...])
    g_cum_row = _dot_hi(grow_ref[0, 0], upper)
    n_sq = max(1, int(math.ceil(math.log2(chunk))) - 1)

    for h in range(DN_HEADS):
        lanes = slice(h * DN_DK, (h + 1) * DN_DK)
        qh = q_ref[:, lanes]
        kh = k_ref[:, lanes]
        vh = v_ref[:, lanes]
        beta = bcol_ref[:, h:h + 1]
        gc = g_cum_col[:, h:h + 1]
        gr = g_cum_row[h:h + 1, :]
        g_last = gr[:, chunk - 1:chunk]
        decay = jnp.where(causal, jnp.exp(jnp.where(causal, gc - gr, 0.0)), 0.0)
        a_mat = jnp.where(strict, beta * _dot_nt_hi(kh, kh) * decay, 0.0)
        inv = eye - a_mat
        a_pow = a_mat
        for _ in range(n_sq):
            a_pow = _dot_hi(a_pow, a_pow)
            inv = inv + _dot_hi(inv, a_pow)
        gam = jnp.exp(gc)
        rhs = jnp.concatenate([beta * vh, (beta * gam) * kh], axis=1)
        sol = _dot_hi(inv, rhs)
        u_blk = sol[:, :DN_DV]
        w_blk = sol[:, DN_DV:]
        qk = _dot_nt_hi(qh, kh) * decay
        qg = qh * gam
        kd = kh * jnp.exp(g_last - gc)
        s_old = s_ref[0, h]
        delta = u_blk - _dot_hi(w_blk, s_old)
        o_ref[:, lanes] = _dot_hi(qg, s_old) + _dot_hi(qk, delta)
        s_ref[0, h] = jnp.exp(g_last) * s_old + _dot_tn_hi(kd, delta)


def _gated_delta(qkv, beta, g, s0, bsz, t_len, chunk):
    nc = t_len // chunk
    g_row = jnp.swapaxes(g.reshape(bsz, nc, chunk, DN_HEADS), 2, 3)
    return pl.pallas_call(
        functools.partial(_gated_delta_kernel, chunk=chunk),
        grid=(bsz, nc),
        in_specs=[pl.BlockSpec((chunk, DN_KEY), lambda b, c: (b * nc + c, 0)),
                  pl.BlockSpec((chunk, DN_KEY), lambda b, c: (b * nc + c, 1)),
                  pl.BlockSpec((chunk, DN_VAL), lambda b, c: (b * nc + c, 2)),
                  pl.BlockSpec((chunk, DN_HEADS), lambda b, c: (b * nc + c, 0)),
                  pl.BlockSpec((chunk, DN_HEADS), lambda b, c: (b * nc + c, 0)),
                  pl.BlockSpec((1, 1, DN_HEADS, chunk), lambda b, c: (b, c, 0, 0)),
                  pl.BlockSpec((1, DN_HEADS, DN_DK, DN_DV), lambda b, c: (b, 0, 0, 0))],
        out_specs=[pl.BlockSpec((chunk, DN_VAL), lambda b, c: (b * nc + c, 0)),
                   pl.BlockSpec((1, DN_HEADS, DN_DK, DN_DV), lambda b, c: (b, 0, 0, 0))],
        out_shape=[jax.ShapeDtypeStruct((bsz * t_len, DN_VAL), F32),
                   jax.ShapeDtypeStruct((bsz, DN_HEADS, DN_DK, DN_DV), F32)],
        compiler_params=_params("arbitrary", "arbitrary"),
        name="gated_delta",
    )(qkv, qkv, qkv, beta, g, g_row, s0)


def _merge_kernel(c_ref, o_ref, z_ref, gc_ref, gd_ref, x_ref, ng_ref, pw2_ref, wo_ref, wout_ref,
                  x1_ref, og_ref):
    for h in range(DN_HEADS):
        lanes = slice(h * DN_DV, (h + 1) * DN_DV)
        oh = o_ref[:, lanes]
        oh = oh * lax.rsqrt(jnp.mean(oh * oh, axis=-1, keepdims=True) + EPS) * ng_ref[...]
        og_ref[:, lanes] = (oh * _silu(z_ref[:, lanes])).astype(BF16)
    y_dn = jnp.dot(og_ref[...], wo_ref[...], preferred_element_type=F32)
    y_conv = jnp.dot(c_ref[...], pw2_ref[...], preferred_element_type=F32)
    mix = _sigmoid(gc_ref[...]) * y_conv + _sigmoid(gd_ref[...]) * y_dn
    x1_ref[...] = x_ref[...] + jnp.dot(mix.astype(BF16), wout_ref[...], preferred_element_type=F32)


def _merge(c, o, proj, x2, ng, pw2, wo, wout):
    n = x2.shape[0]
    tm = min(n, 256)
    return pl.pallas_call(
        _merge_kernel,
        grid=(n // tm,),
        in_specs=[pl.BlockSpec((tm, CONV_DIM), lambda i: (i, 0)),
                  pl.BlockSpec((tm, DN_VAL), lambda i: (i, 0)),
                  pl.BlockSpec((tm, DN_VAL), lambda i: (i, COL_Z // DN_VAL)),
                  pl.BlockSpec((tm, D_MODEL), lambda i: (i, COL_GATE // D_MODEL)),
                  pl.BlockSpec((tm, D_MODEL), lambda i: (i, COL_GATE // D_MODEL + 1)),
                  pl.BlockSpec((tm, D_MODEL), lambda i: (i, 0)),
                  _const_spec((1, DN_DV)),
                  _const_spec((CONV_DIM, D_MODEL)),
                  _const_spec((DN_VAL, D_MODEL)),
                  _const_spec((D_MODEL, D_MODEL))],
        out_specs=pl.BlockSpec((tm, D_MODEL), lambda i: (i, 0)),
        out_shape=jax.ShapeDtypeStruct((n, D_MODEL), F32),
        scratch_shapes=[pltpu.VMEM((tm, DN_VAL), BF16)],
        compiler_params=_params("arbitrary"),
        name="merge",
    )(c, o, proj, proj, proj, x2, ng, pw2, wo, wout)


CAND_ROWS = 8 * 8 + 8 + 8


def _top_rows(x, dst_ref, k):
    for r in range(k):
        m = jnp.max(x, axis=0, keepdims=True)
        dst_ref[r:r + 1, :] = m
        if r + 1 < k:
            x = jnp.where(x == m, -jnp.inf, x)


def _peer_score_kernel(x1_ref, g_ref, wq_ref, keys_ref, xn_ref, s1_ref, a_ref, s2_ref, b_ref, tau_ref,
                       q_scr, c1_scr, c2_scr, cand_scr, top_scr, *, tn):
    x = x1_ref[...]
    xn = (x * lax.rsqrt(jnp.mean(x * x, axis=-1, keepdims=True) + EPS) * g_ref[...]).astype(BF16)
    xn_ref[...] = xn
    q = jnp.dot(xn, wq_ref[...], preferred_element_type=F32)
    for hp in range(2 * PEER_HEADS):
        q_scr[hp] = q[:, hp * PEER_HALF:(hp + 1) * PEER_HALF].astype(BF16)

    def head(h, carry):
        s1 = lax.dot_general(keys_ref[2 * h], q_scr[2 * h], (((1,), (1,)), ((), ())),
                             preferred_element_type=F32)
        s2 = lax.dot_general(keys_ref[2 * h + 1], q_scr[2 * h + 1], (((1,), (1,)), ((), ())),
                             preferred_element_type=F32)
        s1_ref[h] = s1
        s2_ref[h] = s2
        for lc in range(tn // LANES):
            lanes = slice(lc * LANES, (lc + 1) * LANES)
            _top_rows(s1[:, lanes], c1_scr, PEER_TOPK)
            _top_rows(s2[:, lanes], c2_scr, PEER_TOPK)
            c1 = c1_scr[...]
            c2 = c2_scr[...]
            for r1 in range(8):
                cand_scr[r1 * 8:(r1 + 1) * 8, :] = c1[r1:r1 + 1, :] + c2[0:8, :]
            cand_scr[64:72, :] = c1[8:16, :] + c2[0:1, :]
            cand_scr[72:80, :] = c1[0:1, :] + c2[8:16, :]
            _top_rows(cand_scr[...], top_scr, PEER_TOPK)
            top = top_scr[...]
            m0 = top[0:1, :]
            zsum = jnp.sum(jnp.exp(top - m0), axis=0, keepdims=True)
            tau_ref[h, :, lanes] = top[PEER_TOPK - 1:PEER_TOPK, :]
            a_ref[h, :, lanes] = jnp.exp(s1[:, lanes] - c1[0:1, :])
            b_ref[h, :, lanes] = jnp.exp(s2[:, lanes] - c2[0:1, :]) / zsum
        return carry
    lax.fori_loop(0, PEER_HEADS, head, 0)


def _peer_score(x1, g, wq, keys):
    n = x1.shape[0]
    tn = min(n, 256)
    hk = (PEER_HEADS, N_KEYS, n)
    hk_spec = pl.BlockSpec((PEER_HEADS, N_KEYS, tn), lambda i: (0, 0, i))
    return pl.pallas_call(
        functools.partial(_peer_score_kernel, tn=tn),
        grid=(n // tn,),
        in_specs=[pl.BlockSpec((tn, D_MODEL), lambda i: (i, 0)),
                  _const_spec((1, D_MODEL)),
                  _const_spec((D_MODEL, 2 * PEER_HEADS * PEER_HALF)),
                  _const_spec((2 * PEER_HEADS, N_KEYS, PEER_HALF))],
        out_specs=[pl.BlockSpec((tn, D_MODEL), lambda i: (i, 0)),
                   hk_spec, hk_spec, hk_spec, hk_spec,
                   pl.BlockSpec((PEER_HEADS, 1, tn), lambda i: (0, 0, i))],
        out_shape=[jax.ShapeDtypeStruct((n, D_MODEL), BF16),
                   jax.ShapeDtypeStruct(hk, F32), jax.ShapeDtypeStruct(hk, F32),
                   jax.ShapeDtypeStruct(hk, F32), jax.ShapeDtypeStruct(hk, F32),
                   jax.ShapeDtypeStruct((PEER_HEADS, 1, n), F32)],
        scratch_shapes=[pltpu.VMEM((2 * PEER_HEADS, tn, PEER_HALF), BF16),
                        pltpu.VMEM((PEER_TOPK, LANES), F32),
                        pltpu.VMEM((PEER_TOPK, LANES), F32),
                        pltpu.VMEM((CAND_ROWS, LANES), F32),
                        pltpu.VMEM((PEER_TOPK, LANES), F32)],
        compiler_params=_params("arbitrary"),
        name="peer_score",
    )(x1, g, wq, keys)


PEER_ET = 1024
PEER_SUB = 256

_GELU_C = math.sqrt(2.0 / math.pi)


def _gelu_tanh(x):
    return 0.5 * x * (1.0 + jnp.tanh(_GELU_C * (x + 0.044715 * (x * x * x))))


def _peer_dense_kernel(xn_ref, u_ref, vt_ref, s1_ref, a_ref, s2_ref, b_ref, tau_ref, o_ref, act_scr, *, tn):
    @pl.when(pl.program_id(1) == 0)
    def _():
        o_ref[...] = jnp.zeros_like(o_ref)

    for sub in range(PEER_ET // PEER_SUB):
        rows = slice(sub * PEER_SUB, (sub + 1) * PEER_SUB)
        hid = lax.dot_general(u_ref[rows, :], xn_ref[...], (((1,), (1,)), ((), ())),
                              preferred_element_type=F32)
        for il in range(PEER_SUB // N_KEYS):
            i_loc = sub * (PEER_SUB // N_KEYS) + il
            for lc in range(tn // LANES):
                lanes = slice(lc * LANES, (lc + 1) * LANES)
                wgt = None
                for h in range(PEER_HEADS):
                    hit = (s1_ref[h, i_loc:i_loc + 1, lanes] + s2_ref[h, :, lanes]) >= tau_ref[h, :, lanes]
                    term = jnp.where(hit, a_ref[h, i_loc:i_loc + 1, lanes] * b_ref[h, :, lanes], 0.0)
                    wgt = term if wgt is None else wgt + term
                hsub = hid[il * N_KEYS:(il + 1) * N_KEYS, lanes]
                act_scr[sub * PEER_SUB + il * N_KEYS:sub * PEER_SUB + (il + 1) * N_KEYS, lanes] = (
                    _gelu_tanh(hsub) * wgt).astype(BF16)
        o_ref[...] += jnp.dot(vt_ref[:, rows], act_scr[rows, :], preferred_element_type=F32)


def _peer_dense(xn, u_bf, vt_bf, s1, a, s2, b, tau):
    n = xn.shape[0]
    tn = min(n, 512)
    keys_per_step = PEER_ET // N_KEYS
    hk_spec = pl.BlockSpec((PEER_HEADS, N_KEYS, tn), lambda i, e: (0, 0, i))
    hi_spec = pl.BlockSpec((PEER_HEADS, keys_per_step, tn), lambda i, e: (0, e, i))
    return pl.pallas_call(
        functools.partial(_peer_dense_kernel, tn=tn),
        grid=(n // tn, N_EXPERTS // PEER_ET),
        in_specs=[pl.BlockSpec((tn, D_MODEL), lambda i, e: (i, 0)),
                  pl.BlockSpec((PEER_ET, D_MODEL), lambda i, e: (e, 0)),
                  pl.BlockSpec((D_MODEL, PEER_ET), lambda i, e: (0, e)),
                  hi_spec, hi_spec, hk_spec, hk_spec,
                  pl.BlockSpec((PEER_HEADS, 1, tn), lambda i, e: (0, 0, i))],
        out_specs=pl.BlockSpec((D_MODEL, tn), lambda i, e: (0, i)),
        out_shape=jax.ShapeDtypeStruct((D_MODEL, n), F32),
        scratch_shapes=[pltpu.VMEM((PEER_ET, tn), BF16)],
        compiler_params=_params("arbitrary", "arbitrary"),
        name="peer_dense",
    )(xn, u_bf, vt_bf, s1, a, s2, b, tau)


def _final_kernel(x1_ref, pt_ref, g_ref, y_ref):
    x = x1_ref[...] + pt_ref[...].T
    y_ref[...] = x * lax.rsqrt(jnp.mean(x * x, axis=-1, keepdims=True) + EPS) * g_ref[...]


def _final(x1, peer_t, g):
    n = x1.shape[0]
    tn = min(n, 256)
    return pl.pallas_call(
        _final_kernel,
        grid=(n // tn,),
        in_specs=[pl.BlockSpec((tn, D_MODEL), lambda i: (i, 0)),
                  pl.BlockSpec((D_MODEL, tn), lambda i: (0, i)),
                  pl.BlockSpec((1, D_MODEL), lambda i: (0, 0))],
        out_specs=pl.BlockSpec((tn, D_MODEL), lambda i: (i, 0)),
        out_shape=jax.ShapeDtypeStruct((n, D_MODEL), F32),
        compiler_params=_params("arbitrary"),
        name="final",
    )(x1, peer_t, g)


def _prep_weights(norm_mix_g, w_in, conv_dw_w, conv_dw_b, conv_ln_g, conv_ln_b, conv_pw2, dn_conv_w,
                  dn_a_log, dn_dt_bias, dn_norm_g, dn_w_o, w_out, norm_ffn_g, peer_w_q, peer_sub_keys,
                  peer_u, peer_v, norm_final_g):
    w = w_in[0]
    glu_end = 2 * CONV_DIM
    qkv_end = glu_end + DN_CONV_CH
    ba_end = qkv_end + 2 * DN_HEADS
    z_end = ba_end + DN_VAL
    w_pack = jnp.concatenate(
        [w[:, glu_end:qkv_end], w[:, :glu_end], w[:, ba_end:z_end], w[:, z_end:], w[:, qkv_end:ba_end],
         jnp.zeros((D_MODEL, LANES - 2 * DN_HEADS), w.dtype)], axis=1).astype(BF16)
    pad_lo = jnp.zeros((DN_HEADS,), F32)
    pad_hi = jnp.zeros((LANES - 2 * DN_HEADS,), F32)
    return dict(
        norm_mix_g=norm_mix_g[0][None, :], w_pack=w_pack,
        conv_dw_w=conv_dw_w[0], conv_dw_b=conv_dw_b[0][None, :],
        conv_ln_g=conv_ln_g[0][None, :], conv_ln_b=conv_ln_b[0][None, :],
        conv_pw2=conv_pw2[0].astype(BF16), dn_conv_w=dn_conv_w[0],
        alog_pad=jnp.concatenate([pad_lo, dn_a_log[0], pad_hi])[None, :],
        dt_pad=jnp.concatenate([pad_lo, dn_dt_bias[0], pad_hi])[None, :],
        dn_norm_g=dn_norm_g[0][None, :], dn_w_o=dn_w_o[0].astype(BF16), w_out=w_out[0].astype(BF16),
        norm_ffn_g=norm_ffn_g[0][None, :], peer_w_q=peer_w_q[0].astype(BF16),
        peer_keys=peer_sub_keys[0].reshape(2 * PEER_HEADS, N_KEYS, PEER_HALF).astype(BF16),
        peer_u=peer_u[0].astype(BF16), peer_vt=peer_v[0].T.astype(BF16),
        norm_final_g=norm_final_g[None, :])


def _trunk(x, conv_st, dconv_st, dn_st, wd, chunk):
    bsz, t_len, _ = x.shape
    n = bsz * t_len
    x2 = x.reshape(n, D_MODEL)
    proj = _in_proj(x2, wd["norm_mix_g"], wd["w_pack"])
    c_act, conv_new = _conv_branch(proj, conv_st, wd["conv_dw_w"], wd["conv_dw_b"], wd["conv_ln_g"],
                                   wd["conv_ln_b"], bsz, t_len)
    qkv, bg, dconv_new = _dn_prep(proj, dconv_st, wd["dn_conv_w"], wd["alog_pad"], wd["dt_pad"], bsz, t_len)
    beta = bg[:, :DN_HEADS]
    g = bg[:, DN_HEADS:2 * DN_HEADS]
    o, s_new = _gated_delta(qkv, beta, g, dn_st, bsz, t_len, chunk)
    x1 = _merge(c_act, o, proj, x2, wd["dn_norm_g"], wd["conv_pw2"], wd["dn_w_o"], wd["w_out"])
    xn2, s1, a, s2, b, tau = _peer_score(x1, wd["norm_ffn_g"], wd["peer_w_q"], wd["peer_keys"])
    peer_t = _peer_dense(xn2, wd["peer_u"], wd["peer_vt"], s1, a, s2, b, tau)
    y = _final(x1, peer_t, wd["norm_final_g"])
    return y.reshape(bsz, t_len, D_MODEL), conv_new[None], dconv_new[None], s_new[None]


def kernel(x_prompt, x_sample, state_conformer_conv, state_delta_conv, state_delta, norm_mix_g, w_in, conv_dw_w, conv_dw_b, conv_ln_g, conv_ln_b, conv_pw2, dn_conv_w, dn_a_log, dn_dt_bias, dn_norm_g, dn_w_o, w_out, norm_ffn_g, peer_w_q, peer_sub_keys, peer_u, peer_v, norm_final_g):
    wd = _prep_weights(norm_mix_g, w_in, conv_dw_w, conv_dw_b, conv_ln_g, conv_ln_b, conv_pw2, dn_conv_w,
                       dn_a_log, dn_dt_bias, dn_norm_g, dn_w_o, w_out, norm_ffn_g, peer_w_q, peer_sub_keys,
                       peer_u, peer_v, norm_final_g)
    bp = x_prompt.shape[0]
    zc = jnp.zeros((bp, CONV_WIDTH - 1, CONV_DIM), F32)
    zdc = jnp.zeros((bp, DN_CONV_WIDTH - 1, DN_CONV_CH), F32)
    zs = jnp.zeros((bp, DN_HEADS, DN_DK, DN_DV), F32)
    y_p, pc, pdc, ps = _trunk(x_prompt, zc, zdc, zs, wd, 64)
    y_s, sc, sdc, ss = _trunk(x_sample, state_conformer_conv[0], state_delta_conv[0], state_delta[0], wd,
                              x_sample.shape[1])
    return (y_p, y_s, pc, pdc, ps, sc, sdc, ss)
```

```python
import functools
import math

import jax
import jax.numpy as jnp
from jax import lax
from jax.experimental import pallas as pl
from jax.experimental.pallas import tpu as pltpu

F32 = jnp.float32
BF16 = jnp.bfloat16
HI = lax.Precision.HIGHEST

EPS = 1e-6
D_MODEL = 2048
CONV_DIM = 1024
CONV_WIDTH = 31
DN_HEADS = 8
DN_DK = 128
DN_DV = 128
DN_KEY = DN_HEADS * DN_DK
DN_VAL = DN_HEADS * DN_DV
DN_CONV_WIDTH = 4
DN_CONV_CH = 2 * DN_KEY + DN_VAL
PEER_HEADS = 8
PEER_HALF = 128
N_KEYS = 128
N_EXPERTS = N_KEYS * N_KEYS
PEER_TOPK = 16

LANES = 128
SUBLANES = 8
VMEM_LIMIT = 56 * 1024 * 1024

COL_QKV = 0
COL_GLU = COL_QKV + DN_CONV_CH
COL_Z = COL_GLU + 2 * CONV_DIM
COL_GATE = COL_Z + DN_VAL
COL_BA = COL_GATE + 2 * D_MODEL
PROJ_COLS = COL_BA + LANES


def _params(*sem):
    return pltpu.CompilerParams(dimension_semantics=sem, vmem_limit_bytes=VMEM_LIMIT)


def _const_spec(shape):
    nd = len(shape)
    return pl.BlockSpec(shape, lambda *_: (0,) * nd, pipeline_mode=pl.Buffered(1))


def _sigmoid(x):
    return 1.0 / (1.0 + jnp.exp(-x))


def _silu(x):
    return x * _sigmoid(x)


def _in_proj_kernel(x_ref, g_ref, w_ref, o_ref, xn_ref, *, rows):
    @pl.when(pl.program_id(1) == 0)
    def _():
        def body(i, carry):
            r = pl.ds(pl.multiple_of(i * rows, rows), rows)
            x = x_ref[r, :]
            y = x * lax.rsqrt(jnp.mean(x * x, axis=-1, keepdims=True) + EPS)
            xn_ref[r, :] = (y * g_ref[...]).astype(BF16)
            return carry
        lax.fori_loop(0, x_ref.shape[0] // rows, body, 0)

    o_ref[...] = jnp.dot(xn_ref[...], w_ref[...], preferred_element_type=F32)


def _in_proj(x2, g, w_pack):
    n = x2.shape[0]
    tm = min(n, 1024)
    tn = 1152
    return pl.pallas_call(
        functools.partial(_in_proj_kernel, rows=min(tm, 64)),
        grid=(n // tm, PROJ_COLS // tn),
        in_specs=[pl.BlockSpec((tm, D_MODEL), lambda i, j: (i, 0)),
                  pl.BlockSpec((1, D_MODEL), lambda i, j: (0, 0)),
                  pl.BlockSpec((D_MODEL, tn), lambda i, j: (0, j))],
        out_specs=pl.BlockSpec((tm, tn), lambda i, j: (i, j)),
        out_shape=jax.ShapeDtypeStruct((n, PROJ_COLS), F32),
        scratch_shapes=[pltpu.VMEM((tm, D_MODEL), BF16)],
        compiler_params=_params("arbitrary", "arbitrary"),
        name="in_proj",
    )(x2, g, w_pack)


def _conv_rows(xp_ref, w_ref, r0, rows, lanes, first, kw):
    acc = None
    k = 0
    while k < kw:
        off = first + k
        base = (off // SUBLANES) * SUBLANES
        nk = min(kw - k, base + SUBLANES - off)
        extra = SUBLANES if off + nk - 1 > base else 0
        win = xp_ref[pl.ds(pl.multiple_of(r0 + base, SUBLANES), rows + extra), lanes]
        for kk in range(k, k + nk):
            s = first + kk - base
            term = w_ref[kk:kk + 1, lanes] * win[s:s + rows, :]
            acc = term if acc is None else acc + term
        k += nk
    return acc


CONV_HDR = 32


def _conv_branch_kernel(a_ref, b_ref, prev_ref, w_ref, bias_ref, lng_ref, lnb_ref,
                        c_ref, new_ref, xp_ref, *, tt, rows):
    t = pl.program_id(1)
    p = CONV_WIDTH - 1

    @pl.when(t == 0)
    def _():
        xp_ref[CONV_HDR - p:CONV_HDR, :] = prev_ref[0]

    @pl.when(t > 0)
    def _():
        xp_ref[CONV_HDR - p:CONV_HDR, :] = xp_ref[CONV_HDR + tt - p:CONV_HDR + tt, :]

    xp_ref[CONV_HDR:CONV_HDR + tt, :] = a_ref[...] * _sigmoid(b_ref[...])

    def body(i, carry):
        r0 = pl.multiple_of(i * rows, rows)
        y = _conv_rows(xp_ref, w_ref, r0, rows, slice(None), CONV_HDR - p, CONV_WIDTH) + bias_ref[...]
        mu = jnp.mean(y, axis=-1, keepdims=True)
        yc = y - mu
        var = jnp.mean(yc * yc, axis=-1, keepdims=True)
        z = yc * lax.rsqrt(var + EPS) * lng_ref[...] + lnb_ref[...]
        c_ref[pl.ds(r0, rows), :] = _silu(z).astype(BF16)
        return carry
    lax.fori_loop(0, tt // rows, body, 0)

    new_ref[0] = xp_ref[CONV_HDR + tt - p:CONV_HDR + tt, :]


def _conv_branch(proj, conv_prev, w, bias, lng, lnb, bsz, t_len):
    tt = min(t_len, 256)
    nt = t_len // tt
    p = CONV_WIDTH - 1
    blk_a = COL_GLU // CONV_DIM
    return pl.pallas_call(
        functools.partial(_conv_branch_kernel, tt=tt, rows=16),
        grid=(bsz, nt),
        in_specs=[pl.BlockSpec((tt, CONV_DIM), lambda b, t: (b * nt + t, blk_a)),
                  pl.BlockSpec((tt, CONV_DIM), lambda b, t: (b * nt + t, blk_a + 1)),
                  pl.BlockSpec((1, p, CONV_DIM), lambda b, t: (b, 0, 0)),
                  pl.BlockSpec((CONV_WIDTH, CONV_DIM), lambda b, t: (0, 0)),
                  pl.BlockSpec((1, CONV_DIM), lambda b, t: (0, 0)),
                  pl.BlockSpec((1, CONV_DIM), lambda b, t: (0, 0)),
                  pl.BlockSpec((1, CONV_DIM), lambda b, t: (0, 0))],
        out_specs=[pl.BlockSpec((tt, CONV_DIM), lambda b, t: (b * nt + t, 0)),
                   pl.BlockSpec((1, p, CONV_DIM), lambda b, t: (b, 0, 0))],
        out_shape=[jax.ShapeDtypeStruct((bsz * t_len, CONV_DIM), BF16),
                   jax.ShapeDtypeStruct((bsz, p, CONV_DIM), F32)],
        scratch_shapes=[pltpu.VMEM((CONV_HDR + tt, CONV_DIM), F32)],
        compiler_params=_params("arbitrary", "arbitrary"),
        name="conv_branch",
    )(proj, proj, conv_prev, w, bias, lng, lnb)


DN_HDR = 8


def _dn_prep_kernel(x_ref, ba_ref, prev_ref, w_ref, alog_ref, dt_ref,
                    qkv_ref, bg_ref, new_ref, xp_ref, *, tt, rows):
    t = pl.program_id(1)
    p = DN_CONV_WIDTH - 1

    @pl.when(t == 0)
    def _():
        xp_ref[DN_HDR - p:DN_HDR, :] = prev_ref[0]

    @pl.when(t > 0)
    def _():
        xp_ref[DN_HDR - p:DN_HDR, :] = xp_ref[DN_HDR + tt - p:DN_HDR + tt, :]

    xp_ref[DN_HDR:DN_HDR + tt, :] = x_ref[...]

    def body(i, carry):
        r0 = pl.multiple_of(i * rows, rows)
        for hs in range(DN_CONV_CH // LANES):
            lanes = slice(hs * LANES, (hs + 1) * LANES)
            y = _silu(_conv_rows(xp_ref, w_ref, r0, rows, lanes, DN_HDR - p, DN_CONV_WIDTH))
            if hs < 2 * DN_HEADS:
                y = y * lax.rsqrt(jnp.sum(y * y, axis=-1, keepdims=True) + EPS)
                if hs < DN_HEADS:
                    y = y * (DN_DK ** -0.5)
            qkv_ref[pl.ds(r0, rows), lanes] = y
        return carry
    lax.fori_loop(0, tt // rows, body, 0)

    raw = ba_ref[...]
    sp_in = raw + dt_ref[...]
    softplus = jnp.maximum(sp_in, 0.0) + jnp.log(1.0 + jnp.exp(-jnp.abs(sp_in)))
    lane = lax.broadcasted_iota(jnp.int32, raw.shape, 1)
    bg_ref[...] = jnp.where(lane < DN_HEADS, _sigmoid(raw), -jnp.exp(alog_ref[...]) * softplus)

    new_ref[0] = xp_ref[DN_HDR + tt - p:DN_HDR + tt, :]


def _dn_prep(proj, dconv_prev, w, alog_pad, dt_pad, bsz, t_len):
    tt = min(t_len, 256)
    nt = t_len // tt
    p = DN_CONV_WIDTH - 1
    return pl.pallas_call(
        functools.partial(_dn_prep_kernel, tt=tt, rows=16),
        grid=(bsz, nt),
        in_specs=[pl.BlockSpec((tt, DN_CONV_CH), lambda b, t: (b * nt + t, COL_QKV // DN_CONV_CH)),
                  pl.BlockSpec((tt, LANES), lambda b, t: (b * nt + t, COL_BA // LANES)),
                  pl.BlockSpec((1, p, DN_CONV_CH), lambda b, t: (b, 0, 0)),
                  pl.BlockSpec((DN_CONV_WIDTH, DN_CONV_CH), lambda b, t: (0, 0)),
                  pl.BlockSpec((1, LANES), lambda b, t: (0, 0)),
                  pl.BlockSpec((1, LANES), lambda b, t: (0, 0))],
        out_specs=[pl.BlockSpec((tt, DN_CONV_CH), lambda b, t: (b * nt + t, 0)),
                   pl.BlockSpec((tt, LANES), lambda b, t: (b * nt + t, 0)),
                   pl.BlockSpec((1, p, DN_CONV_CH), lambda b, t: (b, 0, 0))],
        out_shape=[jax.ShapeDtypeStruct((bsz * t_len, DN_CONV_CH), F32),
                   jax.ShapeDtypeStruct((bsz * t_len, LANES), F32),
                   jax.ShapeDtypeStruct((bsz, p, DN_CONV_CH), F32)],
        scratch_shapes=[pltpu.VMEM((DN_HDR + tt, DN_CONV_CH), F32)],
        compiler_params=_params("arbitrary", "arbitrary"),
        name="dn_prep",
    )(proj, proj, dconv_prev, w, alog_pad, dt_pad)


def _dot_hi(a, b):
    return jnp.dot(a, b, precision=HI, preferred_element_type=F32)


_NN = (((1,), (0,)), ((), ()))
_NT = (((1,), (1,)), ((), ()))


def _split(a):
    hi = a.astype(BF16)
    lo = (a - hi.astype(F32)).astype(BF16)
    return hi, lo


def _mm3(a, b, dims=_NN):
    a_hi, a_lo = a
    b_hi, b_lo = b
    return (lax.dot_general(a_hi, b_hi, dims, preferred_element_type=F32)
            + lax.dot_general(a_hi, b_lo, dims, preferred_element_type=F32)
            + lax.dot_general(a_lo, b_hi, dims, preferred_element_type=F32))


def _gated_delta_kernel(q_ref, k_ref, v_ref, bcol_ref, gcol_ref, grow_ref, s0_ref, o_ref, s_ref, *, chunk):
    c = pl.program_id(1)

    @pl.when(c == 0)
    def _():
        s_ref[...] = s0_ref[...]

    row = lax.broadcasted_iota(jnp.int32, (chunk, chunk), 0)
    col = lax.broadcasted_iota(jnp.int32, (chunk, chunk), 1)
    causal = row >= col
    strict = row > col
    lower = causal.astype(F32)
    upper = (row <= col).astype(F32)
    eye = (row == col).astype(F32)

    g_cum_col = _dot_hi(lower, gcol_ref[...])
    g_cum_row = _dot_hi(grow_ref[0, 0], upper)
    n_sq = max(1, int(math.ceil(math.log2(chunk))) - 1)

    heads = range(DN_HEADS)
    lanes = [slice(h * DN_DK, (h + 1) * DN_DK) for h in heads]
    q = [q_ref[:, lanes[h]] for h in heads]
    k = [k_ref[:, lanes[h]] for h in heads]
    v = [v_ref[:, lanes[h]] for h in heads]
    beta = [bcol_ref[:, h:h + 1] for h in heads]
    gc = [g_cum_col[:, h:h + 1] for h in heads]
    gr = [g_cum_row[h:h + 1, :] for h in heads]
    g_last = [gr[h][:, chunk - 1:chunk] for h in heads]
    k_sp = [_split(k[h]) for h in heads]
    q_sp = [_split(q[h]) for h in heads]
    s_old = [s_ref[0, h] for h in heads]
    s_sp = [_split(s_old[h]) for h in heads]
    decay = [jnp.where(causal, jnp.exp(jnp.where(causal, gc[h] - gr[h], 0.0)), 0.0) for h in heads]
    a_mat = [jnp.where(strict, beta[h] * _mm3(k_sp[h], k_sp[h], _NT) * decay[h], 0.0) for h in heads]
    qk = [_mm3(q_sp[h], k_sp[h], _NT) * decay[h] for h in heads]
    gam = [jnp.exp(gc[h]) for h in heads]
    qg_s = [_mm3(_split(q[h] * gam[h]), s_sp[h]) for h in heads]
    inv = [eye - a_mat[h] for h in heads]
    a_pow = [_split(a_mat[h]) for h in heads]
    for _ in range(n_sq):
        a_pow = [_split(_mm3(a_pow[h], a_pow[h])) for h in heads]
        inv = [inv[h] + _mm3(_split(inv[h]), a_pow[h]) for h in heads]
    rhs = [jnp.concatenate([beta[h] * v[h], (beta[h] * gam[h]) * k[h]], axis=1) for h in heads]
    sol = [_mm3(_split(inv[h]), _split(rhs[h])) for h in heads]
    delta = [sol[h][:, :DN_DV] - _mm3(_split(sol[h][:, DN_DV:]), s_sp[h]) for h in heads]
    d_sp = [_split(delta[h]) for h in heads]
    for h in heads:
        o_ref[:, lanes[h]] = qg_s[h] + _mm3(_split(qk[h]), d_sp[h])
    for h in heads:
        kd_t = (k[h] * jnp.exp(g_last[h] - gc[h])).T
        s_ref[0, h] = jnp.exp(g_last[h]) * s_old[h] + _mm3(_split(kd_t), d_sp[h])


def _gated_delta(qkv, beta, g, s0, bsz, t_len, chunk):
    nc = t_len // chunk
    g_row = jnp.swapaxes(g.reshape(bsz, nc, chunk, DN_HEADS), 2, 3)
    return pl.pallas_call(
        functools.partial(_gated_delta_kernel, chunk=chunk),
        grid=(bsz, nc),
        in_specs=[pl.BlockSpec((chunk, DN_KEY), lambda b, c: (b * nc + c, 0)),
                  pl.BlockSpec((chunk, DN_KEY), lambda b, c: (b * nc + c, 1)),
                  pl.BlockSpec((chunk, DN_VAL), lambda b, c: (b * nc + c, 2)),
                  pl.BlockSpec((chunk, DN_HEADS), lambda b, c: (b * nc + c, 0)),
                  pl.BlockSpec((chunk, DN_HEADS), lambda b, c: (b * nc + c, 0)),
                  pl.BlockSpec((1, 1, DN_HEADS, chunk), lambda b, c: (b, c, 0, 0)),
                  pl.BlockSpec((1, DN_HEADS, DN_DK, DN_DV), lambda b, c: (b, 0, 0, 0))],
        out_specs=[pl.BlockSpec((chunk, DN_VAL), lambda b, c: (b * nc + c, 0)),
                   pl.BlockSpec((1, DN_HEADS, DN_DK, DN_DV), lambda b, c: (b, 0, 0, 0))],
        out_shape=[jax.ShapeDtypeStruct((bsz * t_len, DN_VAL), F32),
                   jax.ShapeDtypeStruct((bsz, DN_HEADS, DN_DK, DN_DV), F32)],
        compiler_params=_params("arbitrary", "arbitrary"),
        name="gated_delta",
    )(qkv, qkv, qkv, beta, g, g_row, s0)


def _merge_kernel(c_ref, o_ref, z_ref, gc_ref, gd_ref, x_ref, ng_ref, pw2_ref, wo_ref, wout_ref,
                  x1_ref, og_ref):
    for h in range(DN_HEADS):
        lanes = slice(h * DN_DV, (h + 1) * DN_DV)
        oh = o_ref[:, lanes]
        oh = oh * lax.rsqrt(jnp.mean(oh * oh, axis=-1, keepdims=True) + EPS) * ng_ref[...]
        og_ref[:, lanes] = (oh * _silu(z_ref[:, lanes])).astype(BF16)
    y_dn = jnp.dot(og_ref[...], wo_ref[...], preferred_element_type=F32)
    y_conv = jnp.dot(c_ref[...], pw2_ref[...], preferred_element_type=F32)
    mix = _sigmoid(gc_ref[...]) * y_conv + _sigmoid(gd_ref[...]) * y_dn
    x1_ref[...] = x_ref[...] + jnp.dot(mix.astype(BF16), wout_ref[...], preferred_element_type=F32)


def _merge(c, o, proj, x2, ng, pw2, wo, wout):
    n = x2.shape[0]
    tm = min(n, 256)
    return pl.pallas_call(
        _merge_kernel,
        grid=(n // tm,),
        in_specs=[pl.BlockSpec((tm, CONV_DIM), lambda i: (i, 0)),
                  pl.BlockSpec((tm, DN_VAL), lambda i: (i, 0)),
                  pl.BlockSpec((tm, DN_VAL), lambda i: (i, COL_Z // DN_VAL)),
                  pl.BlockSpec((tm, D_MODEL), lambda i: (i, COL_GATE // D_MODEL)),
                  pl.BlockSpec((tm, D_MODEL), lambda i: (i, COL_GATE // D_MODEL + 1)),
                  pl.BlockSpec((tm, D_MODEL), lambda i: (i, 0)),
                  _const_spec((1, DN_DV)),
                  _const_spec((CONV_DIM, D_MODEL)),
                  _const_spec((DN_VAL, D_MODEL)),
                  _const_spec((D_MODEL, D_MODEL))],
        out_specs=pl.BlockSpec((tm, D_MODEL), lambda i: (i, 0)),
        out_shape=jax.ShapeDtypeStruct((n, D_MODEL), F32),
        scratch_shapes=[pltpu.VMEM((tm, DN_VAL), BF16)],
        compiler_params=_params("arbitrary"),
        name="merge",
    )(c, o, proj, proj, proj, x2, ng, pw2, wo, wout)


CAND_ROWS = 8 * 8 + 8 + 8


def _top_rows(x, dst_ref, k):
    for r in range(k):
        m = jnp.max(x, axis=0, keepdims=True)
        dst_ref[r:r + 1, :] = m
        if r + 1 < k:
            x = jnp.where(x == m, -jnp.inf, x)


def _peer_score_kernel(x1_ref, g_ref, wq_ref, keys_ref, xn_ref, s1_ref, a_ref, s2_ref, b_ref, tau_ref,
                       q_scr, c1_scr, c2_scr, cand_scr, top_scr, *, tn):
    x = x1_ref[...]
    xn = (x * lax.rsqrt(jnp.mean(x * x, axis=-1, keepdims=True) + EPS) * g_ref[...]).astype(BF16)
    xn_ref[...] = xn
    q = jnp.dot(xn, wq_ref[...], preferred_element_type=F32)
    for hp in range(2 * PEER_HEADS):
        q_scr[hp] = q[:, hp * PEER_HALF:(hp + 1) * PEER_HALF].astype(BF16)

    def head(h, carry):
        s1 = lax.dot_general(keys_ref[2 * h], q_scr[2 * h], (((1,), (1,)), ((), ())),
                             preferred_element_type=F32)
        s2 = lax.dot_general(keys_ref[2 * h + 1], q_scr[2 * h + 1], (((1,), (1,)), ((), ())),
                             preferred_element_type=F32)
        s1_ref[h] = s1
        s2_ref[h] = s2
        for lc in range(tn // LANES):
            lanes = slice(lc * LANES, (lc + 1) * LANES)
            _top_rows(s1[:, lanes], c1_scr, PEER_TOPK)
            _top_rows(s2[:, lanes], c2_scr, PEER_TOPK)
            c1 = c1_scr[...]
            c2 = c2_scr[...]
            for r1 in range(8):
                cand_scr[r1 * 8:(r1 + 1) * 8, :] = c1[r1:r1 + 1, :] + c2[0:8, :]
            cand_scr[64:72, :] = c1[8:16, :] + c2[0:1, :]
            cand_scr[72:80, :] = c1[0:1, :] + c2[8:16, :]
            _top_rows(cand_scr[...], top_scr, PEER_TOPK)
            top = top_scr[...]
            m0 = top[0:1, :]
            zsum = jnp.sum(jnp.exp(top - m0), axis=0, keepdims=True)
            tau_ref[h, :, lanes] = top[PEER_TOPK - 1:PEER_TOPK, :]
            a_ref[h, :, lanes] = jnp.exp(s1[:, lanes] - c1[0:1, :])
            b_ref[h, :, lanes] = jnp.exp(s2[:, lanes] - c2[0:1, :]) / zsum
        return carry
    lax.fori_loop(0, PEER_HEADS, head, 0)


def _peer_score(x1, g, wq, keys):
    n = x1.shape[0]
    tn = min(n, 256)
    hk = (PEER_HEADS, N_KEYS, n)
    hk_spec = pl.BlockSpec((PEER_HEADS, N_KEYS, tn), lambda i: (0, 0, i))
    return pl.pallas_call(
        functools.partial(_peer_score_kernel, tn=tn),
        grid=(n // tn,),
        in_specs=[pl.BlockSpec((tn, D_MODEL), lambda i: (i, 0)),
                  _const_spec((1, D_MODEL)),
                  _const_spec((D_MODEL, 2 * PEER_HEADS * PEER_HALF)),
                  _const_spec((2 * PEER_HEADS, N_KEYS, PEER_HALF))],
        out_specs=[pl.BlockSpec((tn, D_MODEL), lambda i: (i, 0)),
                   hk_spec, hk_spec, hk_spec, hk_spec,
                   pl.BlockSpec((PEER_HEADS, 1, tn), lambda i: (0, 0, i))],
        out_shape=[jax.ShapeDtypeStruct((n, D_MODEL), BF16),
                   jax.ShapeDtypeStruct(hk, F32), jax.ShapeDtypeStruct(hk, F32),
                   jax.ShapeDtypeStruct(hk, F32), jax.ShapeDtypeStruct(hk, F32),
                   jax.ShapeDtypeStruct((PEER_HEADS, 1, n), F32)],
        scratch_shapes=[pltpu.VMEM((2 * PEER_HEADS, tn, PEER_HALF), BF16),
                        pltpu.VMEM((PEER_TOPK, LANES), F32),
                        pltpu.VMEM((PEER_TOPK, LANES), F32),
                        pltpu.VMEM((CAND_ROWS, LANES), F32),
                        pltpu.VMEM((PEER_TOPK, LANES), F32)],
        compiler_params=_params("arbitrary"),
        name="peer_score",
    )(x1, g, wq, keys)


PEER_ET = 1024
PEER_IG = 2
PEER_JB = 64

_GELU_C = math.sqrt(2.0 / math.pi)


def _gelu_tanh(x):
    return 0.5 * x * (1.0 + jnp.tanh(_GELU_C * (x + 0.044715 * (x * x * x))))


def _peer_dense_kernel(xn_ref, u_ref, vt_ref, s1_ref, a_ref, s2_ref, b_ref, tau_ref, o_ref,
                       act_scr, w_scr, hid_scr, *, tn, steps_per_tile):
    s = pl.program_id(0)
    prev = jnp.maximum(s - 1, 0)
    cur_slot = s % 2
    prev_slot = 1 - cur_slot

    @pl.when(s == 0)
    def _():
        act_scr[1] = jnp.zeros(act_scr.shape[1:], BF16)

    @pl.when(prev % steps_per_tile == 0)
    def _():
        o_ref[...] = jnp.zeros_like(o_ref)

    half = PEER_ET // 2
    quarter = D_MODEL // 4
    n_split = 2 if tn % (4 * LANES) == 0 else 1
    tw = tn // n_split

    def up_proj(p, nh):
        rows, cols = slice(p * half, (p + 1) * half), slice(nh * tw, (nh + 1) * tw)
        hid_scr[rows, cols] = lax.dot_general(u_ref[rows, :], xn_ref[cols, :], _NT,
                                              preferred_element_type=F32)

    def down_proj(p, nh):
        rows, cols = slice(p * quarter, (p + 1) * quarter), slice(nh * tw, (nh + 1) * tw)
        o_ref[rows, cols] += jnp.dot(vt_ref[rows, :], act_scr[prev_slot, :, cols],
                                     preferred_element_type=F32)

    def routing(ig, lc, jb):
        lanes = slice(lc * LANES, (lc + 1) * LANES)
        js = slice(jb * PEER_JB, (jb + 1) * PEER_JB)
        wgt = [None] * PEER_IG
        for h in range(PEER_HEADS):
            s2 = s2_ref[h, js, lanes]
            bb = b_ref[h, js, lanes]
            tau = tau_ref[h, :, lanes]
            for ii in range(PEER_IG):
                i_loc = ig * PEER_IG + ii
                sel = jnp.where(s1_ref[h, i_loc:i_loc + 1, lanes] + s2 >= tau, bb, 0.0)
                term = a_ref[h, i_loc:i_loc + 1, lanes] * sel
                wgt[ii] = term if wgt[ii] is None else wgt[ii] + term
        for ii in range(PEER_IG):
            r0 = (ig * PEER_IG + ii) * N_KEYS + jb * PEER_JB
            w_scr[r0:r0 + PEER_JB, lanes] = wgt[ii]

    def activate(p, nh):
        rows, cols = slice(p * half, (p + 1) * half), slice(nh * tw, (nh + 1) * tw)
        act_scr[cur_slot, rows, cols] = (_gelu_tanh(hid_scr[rows, cols]) * w_scr[rows, cols]).astype(BF16)

    mxu_work = ([functools.partial(up_proj, p, nh) for p in range(2) for nh in range(n_split)]
                + [functools.partial(down_proj, p, nh) for p in range(4) for nh in range(n_split)])
    vpu_work = []
    groups_per_half = half // N_KEYS // PEER_IG
    for p in range(2):
        for nh in range(n_split):
            for lc in range(nh * tw // LANES, (nh + 1) * tw // LANES):
                vpu_work += [functools.partial(routing, ig, lc, jb)
                             for ig in range(p * groups_per_half, (p + 1) * groups_per_half)
                             for jb in range(N_KEYS // PEER_JB)]
            vpu_work.append(functools.partial(activate, p, nh))
    emitted = 0
    for j, work in enumerate(vpu_work):
        while emitted * len(vpu_work) <= j * len(mxu_work) and emitted < len(mxu_work):
            mxu_work[emitted]()
            emitted += 1
        work()
    for work in mxu_work[emitted:]:
        work()


def _peer_dense(xn, u_bf, vt_bf, s1, a, s2, b, tau):
    n = xn.shape[0]
    tn = min(n, 512)
    spt = N_EXPERTS // PEER_ET
    total = (n // tn) * spt
    keys_per_step = PEER_ET // N_KEYS

    def cur(s):
        return jnp.minimum(s, total - 1)

    def prev(s):
        return jnp.maximum(s - 1, 0)

    hk_spec = pl.BlockSpec((PEER_HEADS, N_KEYS, tn), lambda s: (0, 0, cur(s) // spt))
    hi_spec = pl.BlockSpec((PEER_HEADS, keys_per_step, tn), lambda s: (0, cur(s) % spt, cur(s) // spt))
    return pl.pallas_call(
        functools.partial(_peer_dense_kernel, tn=tn, steps_per_tile=spt),
        grid=(total + 1,),
        in_specs=[pl.BlockSpec((tn, D_MODEL), lambda s: (cur(s) // spt, 0)),
                  pl.BlockSpec((PEER_ET, D_MODEL), lambda s: (cur(s) % spt, 0)),
                  pl.BlockSpec((D_MODEL, PEER_ET), lambda s: (0, prev(s) % spt)),
                  hi_spec, hi_spec, hk_spec, hk_spec,
                  pl.BlockSpec((PEER_HEADS, 1, tn), lambda s: (0, 0, cur(s) // spt))],
        out_specs=pl.BlockSpec((D_MODEL, tn), lambda s: (0, prev(s) // spt)),
        out_shape=jax.ShapeDtypeStruct((D_MODEL, n), F32),
        scratch_shapes=[pltpu.VMEM((2, PEER_ET, tn), BF16),
                        pltpu.VMEM((PEER_ET, tn), F32),
                        pltpu.VMEM((PEER_ET, tn), F32)],
        compiler_params=_params("arbitrary"),
        name="peer_dense",
    )(xn, u_bf, vt_bf, s1, a, s2, b, tau)


def _final_kernel(x1_ref, pt_ref, g_ref, y_ref):
    x = x1_ref[...] + pt_ref[...].T
    y_ref[...] = x * lax.rsqrt(jnp.mean(x * x, axis=-1, keepdims=True) + EPS) * g_ref[...]


def _final(x1, peer_t, g):
    n = x1.shape[0]
    tn = min(n, 256)
    return pl.pallas_call(
        _final_kernel,
        grid=(n // tn,),
        in_specs=[pl.BlockSpec((tn, D_MODEL), lambda i: (i, 0)),
                  pl.BlockSpec((D_MODEL, tn), lambda i: (0, i)),
                  pl.BlockSpec((1, D_MODEL), lambda i: (0, 0))],
        out_specs=pl.BlockSpec((tn, D_MODEL), lambda i: (i, 0)),
        out_shape=jax.ShapeDtypeStruct((n, D_MODEL), F32),
        compiler_params=_params("arbitrary"),
        name="final",
    )(x1, peer_t, g)


def _prep_weights(norm_mix_g, w_in, conv_dw_w, conv_dw_b, conv_ln_g, conv_ln_b, conv_pw2, dn_conv_w,
                  dn_a_log, dn_dt_bias, dn_norm_g, dn_w_o, w_out, norm_ffn_g, peer_w_q, peer_sub_keys,
                  peer_u, peer_v, norm_final_g):
    w = w_in[0]
    glu_end = 2 * CONV_DIM
    qkv_end = glu_end + DN_CONV_CH
    ba_end = qkv_end + 2 * DN_HEADS
    z_end = ba_end + DN_VAL
    w_pack = jnp.concatenate(
        [w[:, glu_end:qkv_end], w[:, :glu_end], w[:, ba_end:z_end], w[:, z_end:], w[:, qkv_end:ba_end],
         jnp.zeros((D_MODEL, LANES - 2 * DN_HEADS), w.dtype)], axis=1).astype(BF16)
    pad_lo = jnp.zeros((DN_HEADS,), F32)
    pad_hi = jnp.zeros((LANES - 2 * DN_HEADS,), F32)
    return dict(
        norm_mix_g=norm_mix_g[0][None, :], w_pack=w_pack,
        conv_dw_w=conv_dw_w[0], conv_dw_b=conv_dw_b[0][None, :],
        conv_ln_g=conv_ln_g[0][None, :], conv_ln_b=conv_ln_b[0][None, :],
        conv_pw2=conv_pw2[0].astype(BF16), dn_conv_w=dn_conv_w[0],
        alog_pad=jnp.concatenate([pad_lo, dn_a_log[0], pad_hi])[None, :],
        dt_pad=jnp.concatenate([pad_lo, dn_dt_bias[0], pad_hi])[None, :],
        dn_norm_g=dn_norm_g[0][None, :], dn_w_o=dn_w_o[0].astype(BF16), w_out=w_out[0].astype(BF16),
        norm_ffn_g=norm_ffn_g[0][None, :], peer_w_q=peer_w_q[0].astype(BF16),
        peer_keys=peer_sub_keys[0].reshape(2 * PEER_HEADS, N_KEYS, PEER_HALF).astype(BF16),
        peer_u=peer_u[0].astype(BF16), peer_vt=peer_v[0].T.astype(BF16),
        norm_final_g=norm_final_g[None, :])


def _trunk(x, conv_st, dconv_st, dn_st, wd, chunk):
    bsz, t_len, _ = x.shape
    n = bsz * t_len
    x2 = x.reshape(n, D_MODEL)
    proj = _in_proj(x2, wd["norm_mix_g"], wd["w_pack"])
    c_act, conv_new = _conv_branch(proj, conv_st, wd["conv_dw_w"], wd["conv_dw_b"], wd["conv_ln_g"],
                                   wd["conv_ln_b"], bsz, t_len)
    qkv, bg, dconv_new = _dn_prep(proj, dconv_st, wd["dn_conv_w"], wd["alog_pad"], wd["dt_pad"], bsz, t_len)
    beta = bg[:, :DN_HEADS]
    g = bg[:, DN_HEADS:2 * DN_HEADS]
    o, s_new = _gated_delta(qkv, beta, g, dn_st, bsz, t_len, chunk)
    x1 = _merge(c_act, o, proj, x2, wd["dn_norm_g"], wd["conv_pw2"], wd["dn_w_o"], wd["w_out"])
    xn2, s1, a, s2, b, tau = _peer_score(x1, wd["norm_ffn_g"], wd["peer_w_q"], wd["peer_keys"])
    peer_t = _peer_dense(xn2, wd["peer_u"], wd["peer_vt"], s1, a, s2, b, tau)
    y = _final(x1, peer_t, wd["norm_final_g"])
    return y.reshape(bsz, t_len, D_MODEL), conv_new[None], dconv_new[None], s_new[None]


def kernel(x_prompt, x_sample, state_conformer_conv, state_delta_conv, state_delta, norm_mix_g, w_in, conv_dw_w, conv_dw_b, conv_ln_g, conv_ln_b, conv_pw2, dn_conv_w, dn_a_log, dn_dt_bias, dn_norm_g, dn_w_o, w_out, norm_ffn_g, peer_w_q, peer_sub_keys, peer_u, peer_v, norm_final_g):
    wd = _prep_weights(norm_mix_g, w_in, conv_dw_w, conv_dw_b, conv_ln_g, conv_ln_b, conv_pw2, dn_conv_w,
                       dn_a_log, dn_dt_bias, dn_norm_g, dn_w_o, w_out, norm_ffn_g, peer_w_q, peer_sub_keys,
                       peer_u, peer_v, norm_final_g)
    bp = x_prompt.shape[0]
    zc = jnp.zeros((bp, CONV_WIDTH - 1, CONV_DIM), F32)
    zdc = jnp.zeros((bp, DN_CONV_WIDTH - 1, DN_CONV_CH), F32)
    zs = jnp.zeros((bp, DN_HEADS, DN_DK, DN_DV), F32)
    y_p, pc, pdc, ps = _trunk(x_prompt, zc, zdc, zs, wd, 64)
    y_s, sc, sdc, ss = _trunk(x_sample, state_conformer_conv[0], state_delta_conv[0], state_delta[0], wd,
                              x_sample.shape[1])
    return (y_p, y_s, pc, pdc, ps, sc, sdc, ss)
```

```python
import functools
import math

import jax
import jax.numpy as jnp
from jax import lax
from jax.experimental import pallas as pl
from jax.experimental.pallas import tpu as pltpu

F32 = jnp.float32
BF16 = jnp.bfloat16
HI = lax.Precision.HIGHEST

EPS = 1e-6
D_MODEL = 2048
CONV_DIM = 1024
CONV_WIDTH = 31
DN_HEADS = 8
DN_DK = 128
DN_DV = 128
DN_KEY = DN_HEADS * DN_DK
DN_VAL = DN_HEADS * DN_DV
DN_CONV_WIDTH = 4
DN_CONV_CH = 2 * DN_KEY + DN_VAL
PEER_HEADS = 8
PEER_HALF = 128
N_KEYS = 128
N_EXPERTS = N_KEYS * N_KEYS
PEER_TOPK = 16

LANES = 128
SUBLANES = 8
VMEM_LIMIT = 56 * 1024 * 1024

COL_QKV = 0
COL_GLU = COL_QKV + DN_CONV_CH
COL_Z = COL_GLU + 2 * CONV_DIM
COL_GATE = COL_Z + DN_VAL
COL_BA = COL_GATE + 2 * D_MODEL
PROJ_COLS = COL_BA + LANES


def _params(*sem, flags=None):
    return pltpu.CompilerParams(dimension_semantics=sem, vmem_limit_bytes=VMEM_LIMIT, flags=flags)


def _const_spec(shape):
    nd = len(shape)
    return pl.BlockSpec(shape, lambda *_: (0,) * nd, pipeline_mode=pl.Buffered(1))


def _sigmoid(x):
    return 1.0 / (1.0 + jnp.exp(-x))


def _silu(x):
    return x * _sigmoid(x)


def _in_proj_kernel(x_ref, g_ref, w_ref, o_ref, xn_ref, *, rows):
    @pl.when(pl.program_id(1) == 0)
    def _():
        def body(i, carry):
            r = pl.ds(pl.multiple_of(i * rows, rows), rows)
            x = x_ref[r, :]
            y = x * lax.rsqrt(jnp.mean(x * x, axis=-1, keepdims=True) + EPS)
            xn_ref[r, :] = (y * g_ref[...]).astype(BF16)
            return carry
        lax.fori_loop(0, x_ref.shape[0] // rows, body, 0)

    o_ref[...] = jnp.dot(xn_ref[...], w_ref[...], preferred_element_type=F32)


def _in_proj(x2, g, w_pack):
    n = x2.shape[0]
    tm = min(n, 1024)
    tn = 1152
    return pl.pallas_call(
        functools.partial(_in_proj_kernel, rows=min(tm, 64)),
        grid=(n // tm, PROJ_COLS // tn),
        in_specs=[pl.BlockSpec((tm, D_MODEL), lambda i, j: (i, 0)),
                  pl.BlockSpec((1, D_MODEL), lambda i, j: (0, 0)),
                  pl.BlockSpec((D_MODEL, tn), lambda i, j: (0, j))],
        out_specs=pl.BlockSpec((tm, tn), lambda i, j: (i, j)),
        out_shape=jax.ShapeDtypeStruct((n, PROJ_COLS), F32),
        scratch_shapes=[pltpu.VMEM((tm, D_MODEL), BF16)],
        compiler_params=_params("arbitrary", "arbitrary"),
        name="in_proj",
    )(x2, g, w_pack)


def _conv_rows(xp_ref, w_ref, r0, rows, lanes, first, kw):
    acc = None
    k = 0
    while k < kw:
        off = first + k
        base = (off // SUBLANES) * SUBLANES
        nk = min(kw - k, base + SUBLANES - off)
        extra = SUBLANES if off + nk - 1 > base else 0
        win = xp_ref[pl.ds(pl.multiple_of(r0 + base, SUBLANES), rows + extra), lanes]
        for kk in range(k, k + nk):
            s = first + kk - base
            term = w_ref[kk:kk + 1, lanes] * win[s:s + rows, :]
            acc = term if acc is None else acc + term
        k += nk
    return acc


def _conv_rows_by_residue(xp_ref, w_ref, r0, rows, lanes, first, kw):
    y = None
    for b in range(SUBLANES):
        taps = [k for k in range(kw) if (first + k) % SUBLANES == b]
        if not taps:
            continue
        span = rows + (SUBLANES if b else 0)
        z = None
        for k in taps:
            start = pl.multiple_of(r0 + (first + k) // SUBLANES * SUBLANES, SUBLANES)
            term = w_ref[k:k + 1, lanes] * xp_ref[pl.ds(start, span), lanes]
            z = term if z is None else z + term
        z = z[b:b + rows, :]
        y = z if y is None else y + z
    return y


CONV_HDR = 32
CONV_LANE_CHUNK = 256


def _conv_branch_kernel(a_ref, b_ref, prev_ref, w_ref, bias_ref, lng_ref, lnb_ref,
                        c_ref, new_ref, xp_ref, y_ref, *, tt, rows):
    t = pl.program_id(1)
    p = CONV_WIDTH - 1

    xp_ref[0:CONV_HDR - p, :] = jnp.zeros((CONV_HDR - p, CONV_DIM), F32)

    @pl.when(t == 0)
    def _():
        xp_ref[CONV_HDR - p:CONV_HDR, :] = prev_ref[0]

    @pl.when(t > 0)
    def _():
        xp_ref[CONV_HDR - p:CONV_HDR, :] = xp_ref[CONV_HDR + tt - p:CONV_HDR + tt, :]

    xp_ref[CONV_HDR:CONV_HDR + tt, :] = a_ref[...] * _sigmoid(b_ref[...])

    def body(i, carry):
        r0 = pl.multiple_of(i * rows, rows)
        for lc in range(CONV_DIM // CONV_LANE_CHUNK):
            lanes = slice(lc * CONV_LANE_CHUNK, (lc + 1) * CONV_LANE_CHUNK)
            y_ref[:, lanes] = _conv_rows_by_residue(xp_ref, w_ref, r0, rows, lanes, CONV_HDR - p, CONV_WIDTH)
        y = y_ref[...] + bias_ref[...]
        mu = jnp.mean(y, axis=-1, keepdims=True)
        yc = y - mu
        var = jnp.mean(yc * yc, axis=-1, keepdims=True)
        z = yc * lax.rsqrt(var + EPS) * lng_ref[...] + lnb_ref[...]
        c_ref[pl.ds(r0, rows), :] = _silu(z).astype(BF16)
        return carry
    lax.fori_loop(0, tt // rows, body, 0)

    new_ref[0] = xp_ref[CONV_HDR + tt - p:CONV_HDR + tt, :]


def _conv_branch(proj, conv_prev, w, bias, lng, lnb, bsz, t_len):
    tt = min(t_len, 256)
    nt = t_len // tt
    p = CONV_WIDTH - 1
    blk_a = COL_GLU // CONV_DIM
    rows = min(tt, 32)
    return pl.pallas_call(
        functools.partial(_conv_branch_kernel, tt=tt, rows=rows),
        grid=(bsz, nt),
        in_specs=[pl.BlockSpec((tt, CONV_DIM), lambda b, t: (b * nt + t, blk_a)),
                  pl.BlockSpec((tt, CONV_DIM), lambda b, t: (b * nt + t, blk_a + 1)),
                  pl.BlockSpec((1, p, CONV_DIM), lambda b, t: (b, 0, 0)),
                  pl.BlockSpec((CONV_WIDTH, CONV_DIM), lambda b, t: (0, 0)),
                  pl.BlockSpec((1, CONV_DIM), lambda b, t: (0, 0)),
                  pl.BlockSpec((1, CONV_DIM), lambda b, t: (0, 0)),
                  pl.BlockSpec((1, CONV_DIM), lambda b, t: (0, 0))],
        out_specs=[pl.BlockSpec((tt, CONV_DIM), lambda b, t: (b * nt + t, 0)),
                   pl.BlockSpec((1, p, CONV_DIM), lambda b, t: (b, 0, 0))],
        out_shape=[jax.ShapeDtypeStruct((bsz * t_len, CONV_DIM), BF16),
                   jax.ShapeDtypeStruct((bsz, p, CONV_DIM), F32)],
        scratch_shapes=[pltpu.VMEM((CONV_HDR + tt, CONV_DIM), F32),
                        pltpu.VMEM((rows, CONV_DIM), F32)],
        compiler_params=_params("arbitrary", "arbitrary"),
        name="conv_branch",
    )(proj, proj, conv_prev, w, bias, lng, lnb)


DN_HDR = 8


def _dn_prep_kernel(x_ref, ba_ref, prev_ref, w_ref, alog_ref, dt_ref,
                    qkv_ref, bg_ref, new_ref, xp_ref, *, tt, rows):
    t = pl.program_id(1)
    p = DN_CONV_WIDTH - 1

    xp_ref[0:DN_HDR - p, :] = jnp.zeros((DN_HDR - p, DN_CONV_CH), F32)

    @pl.when(t == 0)
    def _():
        xp_ref[DN_HDR - p:DN_HDR, :] = prev_ref[0]

    @pl.when(t > 0)
    def _():
        xp_ref[DN_HDR - p:DN_HDR, :] = xp_ref[DN_HDR + tt - p:DN_HDR + tt, :]

    xp_ref[DN_HDR:DN_HDR + tt, :] = x_ref[...]

    def body(i, carry):
        r0 = pl.multiple_of(i * rows, rows)
        for hs in range(DN_CONV_CH // LANES):
            lanes = slice(hs * LANES, (hs + 1) * LANES)
            y = _silu(_conv_rows(xp_ref, w_ref, r0, rows, lanes, DN_HDR - p, DN_CONV_WIDTH))
            if hs < 2 * DN_HEADS:
                y = y * lax.rsqrt(jnp.sum(y * y, axis=-1, keepdims=True) + EPS)
                if hs < DN_HEADS:
                    y = y * (DN_DK ** -0.5)
            qkv_ref[pl.ds(r0, rows), lanes] = y
        return carry
    lax.fori_loop(0, tt // rows, body, 0)

    raw = ba_ref[...]
    sp_in = raw + dt_ref[...]
    softplus = jnp.maximum(sp_in, 0.0) + jnp.log(1.0 + jnp.exp(-jnp.abs(sp_in)))
    lane = lax.broadcasted_iota(jnp.int32, raw.shape, 1)
    bg_ref[...] = jnp.where(lane < DN_HEADS, _sigmoid(raw), -jnp.exp(alog_ref[...]) * softplus)

    new_ref[0] = xp_ref[DN_HDR + tt - p:DN_HDR + tt, :]


def _dn_prep(proj, dconv_prev, w, alog_pad, dt_pad, bsz, t_len):
    tt = min(t_len, 256)
    nt = t_len // tt
    p = DN_CONV_WIDTH - 1
    return pl.pallas_call(
        functools.partial(_dn_prep_kernel, tt=tt, rows=16),
        grid=(bsz, nt),
        in_specs=[pl.BlockSpec((tt, DN_CONV_CH), lambda b, t: (b * nt + t, COL_QKV // DN_CONV_CH)),
                  pl.BlockSpec((tt, LANES), lambda b, t: (b * nt + t, COL_BA // LANES)),
                  pl.BlockSpec((1, p, DN_CONV_CH), lambda b, t: (b, 0, 0)),
                  pl.BlockSpec((DN_CONV_WIDTH, DN_CONV_CH), lambda b, t: (0, 0)),
                  pl.BlockSpec((1, LANES), lambda b, t: (0, 0)),
                  pl.BlockSpec((1, LANES), lambda b, t: (0, 0))],
        out_specs=[pl.BlockSpec((tt, DN_CONV_CH), lambda b, t: (b * nt + t, 0)),
                   pl.BlockSpec((tt, LANES), lambda b, t: (b * nt + t, 0)),
                   pl.BlockSpec((1, p, DN_CONV_CH), lambda b, t: (b, 0, 0))],
        out_shape=[jax.ShapeDtypeStruct((bsz * t_len, DN_CONV_CH), F32),
                   jax.ShapeDtypeStruct((bsz * t_len, LANES), F32),
                   jax.ShapeDtypeStruct((bsz, p, DN_CONV_CH), F32)],
        scratch_shapes=[pltpu.VMEM((DN_HDR + tt, DN_CONV_CH), F32)],
        compiler_params=_params("arbitrary", "arbitrary"),
        name="dn_prep",
    )(proj, proj, dconv_prev, w, alog_pad, dt_pad)


def _dot_hi(a, b):
    return jnp.dot(a, b, precision=HI, preferred_element_type=F32)


_NN = (((1,), (0,)), ((), ()))
_NT = (((1,), (1,)), ((), ()))


def _split(a):
    hi = a.astype(BF16)
    lo = (a - hi.astype(F32)).astype(BF16)
    return hi, lo


def _mm3(a, b, dims=_NN):
    a_hi, a_lo = a
    b_hi, b_lo = b
    return (lax.dot_general(a_hi, b_hi, dims, preferred_element_type=F32)
            + lax.dot_general(a_hi, b_lo, dims, preferred_element_type=F32)
            + lax.dot_general(a_lo, b_hi, dims, preferred_element_type=F32))


def _gated_delta_kernel(q_ref, k_ref, v_ref, bcol_ref, gcol_ref, grow_ref, s0_ref, o_ref, s_ref, *, chunk):
    c = pl.program_id(1)

    @pl.when(c == 0)
    def _():
        s_ref[...] = s0_ref[...]

    row = lax.broadcasted_iota(jnp.int32, (chunk, chunk), 0)
    col = lax.broadcasted_iota(jnp.int32, (chunk, chunk), 1)
    causal = row >= col
    strict = row > col
    lower = causal.astype(F32)
    upper = (row <= col).astype(F32)
    eye = (row == col).astype(F32)

    g_cum_col = _dot_hi(lower, gcol_ref[...])
    g_cum_row = _dot_hi(grow_ref[0, 0], upper)
    n_sq = max(1, int(math.ceil(math.log2(chunk))) - 1)

    heads = range(DN_HEADS)
    lanes = [slice(h * DN_DK, (h + 1) * DN_DK) for h in heads]
    q = [q_ref[:, lanes[h]] for h in heads]
    k = [k_ref[:, lanes[h]] for h in heads]
    v = [v_ref[:, lanes[h]] for h in heads]
    beta = [bcol_ref[:, h:h + 1] for h in heads]
    gc = [g_cum_col[:, h:h + 1] for h in heads]
    gr = [g_cum_row[h:h + 1, :] for h in heads]
    g_last = [gr[h][:, chunk - 1:chunk] for h in heads]
    k_sp = [_split(k[h]) for h in heads]
    q_sp = [_split(q[h]) for h in heads]
    s_old = [s_ref[0, h] for h in heads]
    s_sp = [_split(s_old[h]) for h in heads]
    decay = [jnp.where(causal, jnp.exp(jnp.where(causal, gc[h] - gr[h], 0.0)), 0.0) for h in heads]
    a_mat = [jnp.where(strict, beta[h] * _mm3(k_sp[h], k_sp[h], _NT) * decay[h], 0.0) for h in heads]
    qk = [_mm3(q_sp[h], k_sp[h], _NT) * decay[h] for h in heads]
    gam = [jnp.exp(gc[h]) for h in heads]
    qg_s = [_mm3(_split(q[h] * gam[h]), s_sp[h]) for h in heads]
    inv = [eye - a_mat[h] for h in heads]
    a_pow = [_split(a_mat[h]) for h in heads]
    for _ in range(n_sq):
        a_pow = [_split(_mm3(a_pow[h], a_pow[h])) for h in heads]
        inv = [inv[h] + _mm3(_split(inv[h]), a_pow[h]) for h in heads]
    rhs = [jnp.concatenate([beta[h] * v[h], (beta[h] * gam[h]) * k[h]], axis=1) for h in heads]
    sol = [_mm3(_split(inv[h]), _split(rhs[h])) for h in heads]
    delta = [sol[h][:, :DN_DV] - _mm3(_split(sol[h][:, DN_DV:]), s_sp[h]) for h in heads]
    d_sp = [_split(delta[h]) for h in heads]
    for h in heads:
        o_ref[:, lanes[h]] = qg_s[h] + _mm3(_split(qk[h]), d_sp[h])
    for h in heads:
        kd_t = (k[h] * jnp.exp(g_last[h] - gc[h])).T
        s_ref[0, h] = jnp.exp(g_last[h]) * s_old[h] + _mm3(_split(kd_t), d_sp[h])


def _gated_delta(qkv, beta, g, s0, bsz, t_len, chunk):
    nc = t_len // chunk
    g_row = jnp.swapaxes(g.reshape(bsz, nc, chunk, DN_HEADS), 2, 3)
    return pl.pallas_call(
        functools.partial(_gated_delta_kernel, chunk=chunk),
        grid=(bsz, nc),
        in_specs=[pl.BlockSpec((chunk, DN_KEY), lambda b, c: (b * nc + c, 0)),
                  pl.BlockSpec((chunk, DN_KEY), lambda b, c: (b * nc + c, 1)),
                  pl.BlockSpec((chunk, DN_VAL), lambda b, c: (b * nc + c, 2)),
                  pl.BlockSpec((chunk, DN_HEADS), lambda b, c: (b * nc + c, 0)),
                  pl.BlockSpec((chunk, DN_HEADS), lambda b, c: (b * nc + c, 0)),
                  pl.BlockSpec((1, 1, DN_HEADS, chunk), lambda b, c: (b, c, 0, 0)),
                  pl.BlockSpec((1, DN_HEADS, DN_DK, DN_DV), lambda b, c: (b, 0, 0, 0))],
        out_specs=[pl.BlockSpec((chunk, DN_VAL), lambda b, c: (b * nc + c, 0)),
                   pl.BlockSpec((1, DN_HEADS, DN_DK, DN_DV), lambda b, c: (b, 0, 0, 0))],
        out_shape=[jax.ShapeDtypeStruct((bsz * t_len, DN_VAL), F32),
                   jax.ShapeDtypeStruct((bsz, DN_HEADS, DN_DK, DN_DV), F32)],
        compiler_params=_params("arbitrary", "arbitrary"),
        name="gated_delta",
    )(qkv, qkv, qkv, beta, g, g_row, s0)


def _merge_kernel(c_ref, o_ref, z_ref, gc_ref, gd_ref, x_ref, ng_ref, pw2_ref, wo_ref, wout_ref,
                  x1_ref, og_ref):
    for h in range(DN_HEADS):
        lanes = slice(h * DN_DV, (h + 1) * DN_DV)
        oh = o_ref[:, lanes]
        oh = oh * lax.rsqrt(jnp.mean(oh * oh, axis=-1, keepdims=True) + EPS) * ng_ref[...]
        og_ref[:, lanes] = (oh * _silu(z_ref[:, lanes])).astype(BF16)
    y_dn = jnp.dot(og_ref[...], wo_ref[...], preferred_element_type=F32)
    y_conv = jnp.dot(c_ref[...], pw2_ref[...], preferred_element_type=F32)
    mix = _sigmoid(gc_ref[...]) * y_conv + _sigmoid(gd_ref[...]) * y_dn
    x1_ref[...] = x_ref[...] + jnp.dot(mix.astype(BF16), wout_ref[...], preferred_element_type=F32)


def _merge(c, o, proj, x2, ng, pw2, wo, wout):
    n = x2.shape[0]
    tm = min(n, 256)
    return pl.pallas_call(
        _merge_kernel,
        grid=(n // tm,),
        in_specs=[pl.BlockSpec((tm, CONV_DIM), lambda i: (i, 0)),
                  pl.BlockSpec((tm, DN_VAL), lambda i: (i, 0)),
                  pl.BlockSpec((tm, DN_VAL), lambda i: (i, COL_Z // DN_VAL)),
                  pl.BlockSpec((tm, D_MODEL), lambda i: (i, COL_GATE // D_MODEL)),
                  pl.BlockSpec((tm, D_MODEL), lambda i: (i, COL_GATE // D_MODEL + 1)),
                  pl.BlockSpec((tm, D_MODEL), lambda i: (i, 0)),
                  _const_spec((1, DN_DV)),
                  _const_spec((CONV_DIM, D_MODEL)),
                  _const_spec((DN_VAL, D_MODEL)),
                  _const_spec((D_MODEL, D_MODEL))],
        out_specs=pl.BlockSpec((tm, D_MODEL), lambda i: (i, 0)),
        out_shape=jax.ShapeDtypeStruct((n, D_MODEL), F32),
        scratch_shapes=[pltpu.VMEM((tm, DN_VAL), BF16)],
        compiler_params=_params("arbitrary"),
        name="merge",
    )(c, o, proj, proj, proj, x2, ng, pw2, wo, wout)


CAND_ROWS = 8 * 8 + 8 + 8


PEER_ET = 512
PEER_KEY_BLOCK = SUBLANES
PEER_KEYS_PER_STEP = PEER_ET // N_KEYS
PEER_KEY_ROWS = N_KEYS // PEER_KEYS_PER_STEP * PEER_KEY_BLOCK


def _store_key_groups(dst_ref, h, lanes, x):
    for g in range(N_KEYS // PEER_KEYS_PER_STEP):
        dst_ref[h, g * PEER_KEY_BLOCK:g * PEER_KEY_BLOCK + PEER_KEYS_PER_STEP, lanes] = (
            x[g * PEER_KEYS_PER_STEP:(g + 1) * PEER_KEYS_PER_STEP, :])


def _top_rows(xs, dsts, k):
    slabs = [[x[i:i + SUBLANES, :] for i in range(0, x.shape[0], SUBLANES)] for x in xs]
    for r in range(k):
        for c, (dst_ref, lanes) in enumerate(dsts):
            level = slabs[c]
            while len(level) > 1:
                level = [jnp.maximum(level[i], level[i + 1]) if i + 1 < len(level) else level[i]
                         for i in range(0, len(level), 2)]
            m = jnp.max(level[0], axis=0, keepdims=True)
            dst_ref[r:r + 1, lanes] = m
            if r + 1 < k:
                slabs[c] = [jnp.where(s == m, -jnp.inf, s) for s in slabs[c]]


def _peer_score_kernel(x1_ref, g_ref, wq_ref, keys_ref, xn_ref, s1_ref, a_ref, s2_ref, b_ref, tau_ref,
                       q_scr, c1_scr, c2_scr, cand_scr, top_scr, *, tn):
    x = x1_ref[...]
    xn = x * lax.rsqrt(jnp.mean(x * x, axis=-1, keepdims=True) + EPS) * g_ref[...]
    xn_ref[...] = xn.T.astype(BF16)
    q = jnp.dot(xn.astype(BF16), wq_ref[...], preferred_element_type=F32)
    for hp in range(2 * PEER_HEADS):
        q_scr[hp] = q[:, hp * PEER_HALF:(hp + 1) * PEER_HALF].astype(BF16)

    def head(h, carry):
        s1 = lax.dot_general(keys_ref[2 * h], q_scr[2 * h], (((1,), (1,)), ((), ())),
                             preferred_element_type=F32)
        s2 = lax.dot_general(keys_ref[2 * h + 1], q_scr[2 * h + 1], (((1,), (1,)), ((), ())),
                             preferred_element_type=F32)
        s1_ref[h] = jnp.zeros(s1_ref.shape[1:], F32)
        a_ref[h] = jnp.zeros(a_ref.shape[1:], F32)
        _store_key_groups(s1_ref, h, slice(None), s1)
        s2_ref[h] = s2
        chunks = [slice(lc * LANES, (lc + 1) * LANES) for lc in range(tn // LANES)]
        for lanes in chunks:
            _top_rows([s1[:, lanes], s2[:, lanes]], [(c1_scr, lanes), (c2_scr, lanes)], PEER_TOPK)
        for lanes in chunks:
            c1 = c1_scr[:, lanes]
            c2 = c2_scr[:, lanes]
            for r1 in range(8):
                cand_scr[r1 * 8:(r1 + 1) * 8, lanes] = c1[r1:r1 + 1, :] + c2[0:8, :]
            cand_scr[64:72, lanes] = c1[8:16, :] + c2[0:1, :]
            cand_scr[72:80, lanes] = c1[0:1, :] + c2[8:16, :]
        _top_rows([cand_scr[:, lanes] for lanes in chunks], [(top_scr, lanes) for lanes in chunks], PEER_TOPK)
        for lanes in chunks:
            top = top_scr[:, lanes]
            m0 = top[0:1, :]
            zsum = jnp.sum(jnp.exp(top - m0), axis=0, keepdims=True)
            tau_ref[h, :, lanes] = top[PEER_TOPK - 1:PEER_TOPK, :]
            _store_key_groups(a_ref, h, lanes, jnp.exp(s1[:, lanes] - c1_scr[0:1, lanes]))
            b_ref[h, :, lanes] = jnp.exp(s2[:, lanes] - c2_scr[0:1, lanes]) / zsum
        return carry
    lax.fori_loop(0, PEER_HEADS, head, 0)


def _peer_score(x1, g, wq, keys):
    n = x1.shape[0]
    tn = min(n, 256)
    hk = (PEER_HEADS, N_KEYS, n)
    hk_spec = pl.BlockSpec((PEER_HEADS, N_KEYS, tn), lambda i: (0, 0, i))
    hi = (PEER_HEADS, PEER_KEY_ROWS, n)
    hi_spec = pl.BlockSpec((PEER_HEADS, PEER_KEY_ROWS, tn), lambda i: (0, 0, i))
    return pl.pallas_call(
        functools.partial(_peer_score_kernel, tn=tn),
        grid=(n // tn,),
        in_specs=[pl.BlockSpec((tn, D_MODEL), lambda i: (i, 0)),
                  _const_spec((1, D_MODEL)),
                  _const_spec((D_MODEL, 2 * PEER_HEADS * PEER_HALF)),
                  _const_spec((2 * PEER_HEADS, N_KEYS, PEER_HALF))],
        out_specs=[pl.BlockSpec((D_MODEL, tn), lambda i: (0, i)),
                   hi_spec, hi_spec, hk_spec, hk_spec,
                   pl.BlockSpec((PEER_HEADS, 1, tn), lambda i: (0, 0, i))],
        out_shape=[jax.ShapeDtypeStruct((D_MODEL, n), BF16),
                   jax.ShapeDtypeStruct(hi, F32), jax.ShapeDtypeStruct(hi, F32),
                   jax.ShapeDtypeStruct(hk, F32), jax.ShapeDtypeStruct(hk, F32),
                   jax.ShapeDtypeStruct((PEER_HEADS, 1, n), F32)],
        scratch_shapes=[pltpu.VMEM((2 * PEER_HEADS, tn, PEER_HALF), BF16),
                        pltpu.VMEM((PEER_TOPK, tn), F32),
                        pltpu.VMEM((PEER_TOPK, tn), F32),
                        pltpu.VMEM((CAND_ROWS, tn), F32),
                        pltpu.VMEM((PEER_TOPK, tn), F32)],
        compiler_params=_params("arbitrary"),
        name="peer_score",
    )(x1, g, wq, keys)


PEER_IG = 2
PEER_JB = 64

_GELU_C = math.sqrt(2.0 / math.pi)


def _gelu_tanh(x):
    return 0.5 * x * (1.0 + jnp.tanh(_GELU_C * (x + 0.044715 * (x * x * x))))


def _peer_dense_kernel(xn_ref, u_ref, vt_ref, s1_ref, a_ref, s2_ref, b_ref, tau_ref, o_ref,
                       act_scr, w_scr, hid_scr, *, tn, steps_per_tile):
    s = pl.program_id(0)
    prev = jnp.maximum(s - 1, 0)
    cur_slot = s % 2
    prev_slot = 1 - cur_slot

    @pl.when(s == 0)
    def _():
        act_scr[1] = jnp.zeros(act_scr.shape[1:], BF16)

    @pl.when(prev % steps_per_tile == 0)
    def _():
        o_ref[...] = jnp.zeros_like(o_ref)

    keys_per_step = PEER_ET // N_KEYS
    up_w = min(tn, 2 * LANES)
    down_w = min(tn, 4 * LANES)
    down_rows = D_MODEL // 4

    def up_proj(nq):
        cols = slice(nq * up_w, (nq + 1) * up_w)
        hid_scr[:, cols] = jnp.dot(u_ref[...], xn_ref[:, cols], preferred_element_type=F32)

    def down_proj(p, nh):
        rows, cols = slice(p * down_rows, (p + 1) * down_rows), slice(nh * down_w, (nh + 1) * down_w)
        o_ref[rows, cols] += jnp.dot(vt_ref[rows, :], act_scr[prev_slot, :, cols],
                                     preferred_element_type=F32)

    def routing(ig, lc, jb):
        lanes = slice(lc * LANES, (lc + 1) * LANES)
        js = slice(jb * PEER_JB, (jb + 1) * PEER_JB)
        wgt = [None] * PEER_IG
        for h in range(PEER_HEADS):
            s2 = s2_ref[h, js, lanes]
            bb = b_ref[h, js, lanes]
            tau = tau_ref[h, :, lanes]
            for ii in range(PEER_IG):
                i_loc = ig * PEER_IG + ii
                sel = jnp.where(s1_ref[h, i_loc:i_loc + 1, lanes] + s2 >= tau, bb, 0.0)
                term = a_ref[h, i_loc:i_loc + 1, lanes] * sel
                wgt[ii] = term if wgt[ii] is None else wgt[ii] + term
        for ii in range(PEER_IG):
            r0 = (ig * PEER_IG + ii) * N_KEYS + jb * PEER_JB
            w_scr[r0:r0 + PEER_JB, lanes] = wgt[ii]

    def activate(nq):
        cols = slice(nq * up_w, (nq + 1) * up_w)
        act_scr[cur_slot, :, cols] = (_gelu_tanh(hid_scr[:, cols]) * w_scr[:, cols]).astype(BF16)

    mxu_work = ([functools.partial(up_proj, nq) for nq in range(tn // up_w)]
                + [functools.partial(down_proj, p, nh) for p in range(4) for nh in range(tn // down_w)])
    vpu_work = []
    for nq in range(tn // up_w):
        for lc in range(nq * up_w // LANES, (nq + 1) * up_w // LANES):
            vpu_work += [functools.partial(routing, ig, lc, jb)
                         for ig in range(keys_per_step // PEER_IG)
                         for jb in range(N_KEYS // PEER_JB)]
        vpu_work.append(functools.partial(activate, nq))
    emitted = 0
    for j, work in enumerate(vpu_work):
        while emitted * len(vpu_work) <= j * len(mxu_work) and emitted < len(mxu_work):
            mxu_work[emitted]()
            emitted += 1
        work()
    for work in mxu_work[emitted:]:
        work()


def _peer_dense(xn, u_bf, vt_bf, s1, a, s2, b, tau):
    n = xn.shape[1]
    tn = min(n, 1024)
    spt = N_EXPERTS // PEER_ET
    total = (n // tn) * spt

    def cur(s):
        return jnp.minimum(s, total - 1)

    def prev(s):
        return jnp.maximum(s - 1, 0)

    hk_spec = pl.BlockSpec((PEER_HEADS, N_KEYS, tn), lambda s: (0, 0, cur(s) // spt),
                           pipeline_mode=pl.Buffered(1))
    hi_spec = pl.BlockSpec((PEER_HEADS, PEER_KEY_BLOCK, tn), lambda s: (0, cur(s) % spt, cur(s) // spt))
    return pl.pallas_call(
        functools.partial(_peer_dense_kernel, tn=tn, steps_per_tile=spt),
        grid=(total + 1,),
        in_specs=[pl.BlockSpec((D_MODEL, tn), lambda s: (0, cur(s) // spt)),
                  pl.BlockSpec((PEER_ET, D_MODEL), lambda s: (cur(s) % spt, 0)),
                  pl.BlockSpec((D_MODEL, PEER_ET), lambda s: (0, prev(s) % spt)),
                  hi_spec, hi_spec, hk_spec, hk_spec,
                  pl.BlockSpec((PEER_HEADS, 1, tn), lambda s: (0, 0, cur(s) // spt))],
        out_specs=pl.BlockSpec((D_MODEL, tn), lambda s: (0, prev(s) // spt)),
        out_shape=jax.ShapeDtypeStruct((D_MODEL, n), F32),
        scratch_shapes=[pltpu.VMEM((2, PEER_ET, tn), BF16),
                        pltpu.VMEM((PEER_ET, tn), F32),
                        pltpu.VMEM((PEER_ET, tn), F32)],
        compiler_params=_params("arbitrary"),
        name="peer_dense",
    )(xn, u_bf, vt_bf, s1, a, s2, b, tau)


def _final_kernel(x1_ref, pt_ref, g_ref, y_ref):
    x = x1_ref[...] + pt_ref[...].T
    y_ref[...] = x * lax.rsqrt(jnp.mean(x * x, axis=-1, keepdims=True) + EPS) * g_ref[...]


def _final(x1, peer_t, g):
    n = x1.shape[0]
    tn = min(n, 256)
    return pl.pallas_call(
        _final_kernel,
        grid=(n // tn,),
        in_specs=[pl.BlockSpec((tn, D_MODEL), lambda i: (i, 0)),
                  pl.BlockSpec((D_MODEL, tn), lambda i: (0, i)),
                  pl.BlockSpec((1, D_MODEL), lambda i: (0, 0))],
        out_specs=pl.BlockSpec((tn, D_MODEL), lambda i: (i, 0)),
        out_shape=jax.ShapeDtypeStruct((n, D_MODEL), F32),
        compiler_params=_params("arbitrary"),
        name="final",
    )(x1, peer_t, g)


def _prep_weights(norm_mix_g, w_in, conv_dw_w, conv_dw_b, conv_ln_g, conv_ln_b, conv_pw2, dn_conv_w,
                  dn_a_log, dn_dt_bias, dn_norm_g, dn_w_o, w_out, norm_ffn_g, peer_w_q, peer_sub_keys,
                  peer_u, peer_v, norm_final_g):
    w = w_in[0]
    glu_end = 2 * CONV_DIM
    qkv_end = glu_end + DN_CONV_CH
    ba_end = qkv_end + 2 * DN_HEADS
    z_end = ba_end + DN_VAL
    w_pack = jnp.concatenate(
        [w[:, glu_end:qkv_end], w[:, :glu_end], w[:, ba_end:z_end], w[:, z_end:], w[:, qkv_end:ba_end],
         jnp.zeros((D_MODEL, LANES - 2 * DN_HEADS), w.dtype)], axis=1).astype(BF16)
    pad_lo = jnp.zeros((DN_HEADS,), F32)
    pad_hi = jnp.zeros((LANES - 2 * DN_HEADS,), F32)
    return dict(
        norm_mix_g=norm_mix_g[0][None, :], w_pack=w_pack,
        conv_dw_w=conv_dw_w[0], conv_dw_b=conv_dw_b[0][None, :],
        conv_ln_g=conv_ln_g[0][None, :], conv_ln_b=conv_ln_b[0][None, :],
        conv_pw2=conv_pw2[0].astype(BF16), dn_conv_w=dn_conv_w[0],
        alog_pad=jnp.concatenate([pad_lo, dn_a_log[0], pad_hi])[None, :],
        dt_pad=jnp.concatenate([pad_lo, dn_dt_bias[0], pad_hi])[None, :],
        dn_norm_g=dn_norm_g[0][None, :], dn_w_o=dn_w_o[0].astype(BF16), w_out=w_out[0].astype(BF16),
        norm_ffn_g=norm_ffn_g[0][None, :], peer_w_q=peer_w_q[0].astype(BF16),
        peer_keys=peer_sub_keys[0].reshape(2 * PEER_HEADS, N_KEYS, PEER_HALF).astype(BF16),
        peer_u=peer_u[0].astype(BF16), peer_vt=peer_v[0].T.astype(BF16),
        norm_final_g=norm_final_g[None, :])


def _trunk(x, conv_st, dconv_st, dn_st, wd, chunk):
    bsz, t_len, _ = x.shape
    n = bsz * t_len
    x2 = x.reshape(n, D_MODEL)
    proj = _in_proj(x2, wd["norm_mix_g"], wd["w_pack"])
    c_act, conv_new = _conv_branch(proj, conv_st, wd["conv_dw_w"], wd["conv_dw_b"], wd["conv_ln_g"],
                                   wd["conv_ln_b"], bsz, t_len)
    qkv, bg, dconv_new = _dn_prep(proj, dconv_st, wd["dn_conv_w"], wd["alog_pad"], wd["dt_pad"], bsz, t_len)
    beta = bg[:, :DN_HEADS]
    g = bg[:, DN_HEADS:2 * DN_HEADS]
    o, s_new = _gated_delta(qkv, beta, g, dn_st, bsz, t_len, chunk)
    x1 = _merge(c_act, o, proj, x2, wd["dn_norm_g"], wd["conv_pw2"], wd["dn_w_o"], wd["w_out"])
    xn2, s1, a, s2, b, tau = _peer_score(x1, wd["norm_ffn_g"], wd["peer_w_q"], wd["peer_keys"])
    peer_t = _peer_dense(xn2, wd["peer_u"], wd["peer_vt"], s1, a, s2, b, tau)
    y = _final(x1, peer_t, wd["norm_final_g"])
    return y.reshape(bsz, t_len, D_MODEL), conv_new[None], dconv_new[None], s_new[None]


def kernel(x_prompt, x_sample, state_conformer_conv, state_delta_conv, state_delta, norm_mix_g, w_in, conv_dw_w, conv_dw_b, conv_ln_g, conv_ln_b, conv_pw2, dn_conv_w, dn_a_log, dn_dt_bias, dn_norm_g, dn_w_o, w_out, norm_ffn_g, peer_w_q, peer_sub_keys, peer_u, peer_v, norm_final_g):
    wd = _prep_weights(norm_mix_g, w_in, conv_dw_w, conv_dw_b, conv_ln_g, conv_ln_b, conv_pw2, dn_conv_w,
                       dn_a_log, dn_dt_bias, dn_norm_g, dn_w_o, w_out, norm_ffn_g, peer_w_q, peer_sub_keys,
                       peer_u, peer_v, norm_final_g)
    bp = x_prompt.shape[0]
    zc = jnp.zeros((bp, CONV_WIDTH - 1, CONV_DIM), F32)
    zdc = jnp.zeros((bp, DN_CONV_WIDTH - 1, DN_CONV_CH), F32)
    zs = jnp.zeros((bp, DN_HEADS, DN_DK, DN_DV), F32)
    y_p, pc, pdc, ps = _trunk(x_prompt, zc, zdc, zs, wd, 64)
    y_s, sc, sdc, ss = _trunk(x_sample, state_conformer_conv[0], state_delta_conv[0], state_delta[0], wd,
                              x_sample.shape[1])
    return (y_p, y_s, pc, pdc, ps, sc, sdc, ss)
```

```python
import functools
import math

import jax
import jax.numpy as jnp
from jax import lax
from jax.experimental import pallas as pl
from jax.experimental.pallas import tpu as pltpu

F32 = jnp.float32
BF16 = jnp.bfloat16
HI = lax.Precision.HIGHEST

EPS = 1e-6
D_MODEL = 2048
CONV_DIM = 1024
CONV_WIDTH = 31
DN_HEADS = 8
DN_DK = 128
DN_DV = 128
DN_KEY = DN_HEADS * DN_DK
DN_VAL = DN_HEADS * DN_DV
DN_CONV_WIDTH = 4
DN_CONV_CH = 2 * DN_KEY + DN_VAL
PEER_HEADS = 8
PEER_HALF = 128
N_KEYS = 128
N_EXPERTS = N_KEYS * N_KEYS
PEER_TOPK = 16

LANES = 128
SUBLANES = 8
VMEM_LIMIT = 56 * 1024 * 1024

COL_QKV = 0
COL_GLU = COL_QKV + DN_CONV_CH
COL_Z = COL_GLU + 2 * CONV_DIM
COL_GATE = COL_Z + DN_VAL
COL_BA = COL_GATE + 2 * D_MODEL
PROJ_COLS = COL_BA + LANES


def _params(*sem, flags=None):
    return pltpu.CompilerParams(dimension_semantics=sem, vmem_limit_bytes=VMEM_LIMIT, flags=flags)


def _const_spec(shape):
    nd = len(shape)
    return pl.BlockSpec(shape, lambda *_: (0,) * nd, pipeline_mode=pl.Buffered(1))


def _sigmoid(x):
    return 1.0 / (1.0 + jnp.exp(-x))


def _silu(x):
    return x * _sigmoid(x)


def _in_proj_kernel(x_ref, g_ref, w_ref, o_ref, xn_ref, *, rows):
    @pl.when(pl.program_id(1) == 0)
    def _():
        def body(i, carry):
            r = pl.ds(pl.multiple_of(i * rows, rows), rows)
            x = x_ref[r, :]
            y = x * lax.rsqrt(jnp.mean(x * x, axis=-1, keepdims=True) + EPS)
            xn_ref[r, :] = (y * g_ref[...]).astype(BF16)
            return carry
        lax.fori_loop(0, x_ref.shape[0] // rows, body, 0)

    o_ref[...] = jnp.dot(xn_ref[...], w_ref[...], preferred_element_type=F32)


def _in_proj(x2, g, w_pack):
    n = x2.shape[0]
    tm = min(n, 1024)
    tn = 1152
    return pl.pallas_call(
        functools.partial(_in_proj_kernel, rows=min(tm, 64)),
        grid=(n // tm, PROJ_COLS // tn),
        in_specs=[pl.BlockSpec((tm, D_MODEL), lambda i, j: (i, 0)),
                  pl.BlockSpec((1, D_MODEL), lambda i, j: (0, 0)),
                  pl.BlockSpec((D_MODEL, tn), lambda i, j: (0, j))],
        out_specs=pl.BlockSpec((tm, tn), lambda i, j: (i, j)),
        out_shape=jax.ShapeDtypeStruct((n, PROJ_COLS), F32),
        scratch_shapes=[pltpu.VMEM((tm, D_MODEL), BF16)],
        compiler_params=_params("arbitrary", "arbitrary"),
        name="in_proj",
    )(x2, g, w_pack)


def _conv_rows(xp_ref, w_ref, r0, rows, lanes, first, kw):
    acc = None
    k = 0
    while k < kw:
        off = first + k
        base = (off // SUBLANES) * SUBLANES
        nk = min(kw - k, base + SUBLANES - off)
        extra = SUBLANES if off + nk - 1 > base else 0
        win = xp_ref[pl.ds(pl.multiple_of(r0 + base, SUBLANES), rows + extra), lanes]
        for kk in range(k, k + nk):
            s = first + kk - base
            term = w_ref[kk:kk + 1, lanes] * win[s:s + rows, :]
            acc = term if acc is None else acc + term
        k += nk
    return acc


def _conv_rows_by_residue(xp_ref, w_ref, r0, rows, lanes, first, kw):
    y = None
    for b in range(SUBLANES):
        taps = [k for k in range(kw) if (first + k) % SUBLANES == b]
        if not taps:
            continue
        span = rows + (SUBLANES if b else 0)
        z = None
        for k in taps:
            start = pl.multiple_of(r0 + (first + k) // SUBLANES * SUBLANES, SUBLANES)
            term = w_ref[k:k + 1, lanes] * xp_ref[pl.ds(start, span), lanes]
            z = term if z is None else z + term
        z = z[b:b + rows, :]
        y = z if y is None else y + z
    return y


CONV_HDR = 32
CONV_LANE_CHUNK = 256


def _conv_branch_kernel(a_ref, b_ref, prev_ref, w_ref, bias_ref, lng_ref, lnb_ref,
                        c_ref, new_ref, xp_ref, y_ref, *, tt, rows):
    t = pl.program_id(1)
    p = CONV_WIDTH - 1

    xp_ref[0:CONV_HDR - p, :] = jnp.zeros((CONV_HDR - p, CONV_DIM), F32)

    @pl.when(t == 0)
    def _():
        xp_ref[CONV_HDR - p:CONV_HDR, :] = prev_ref[0]

    @pl.when(t > 0)
    def _():
        xp_ref[CONV_HDR - p:CONV_HDR, :] = xp_ref[CONV_HDR + tt - p:CONV_HDR + tt, :]

    xp_ref[CONV_HDR:CONV_HDR + tt, :] = a_ref[...] * _sigmoid(b_ref[...])

    def body(i, carry):
        r0 = pl.multiple_of(i * rows, rows)
        for lc in range(CONV_DIM // CONV_LANE_CHUNK):
            lanes = slice(lc * CONV_LANE_CHUNK, (lc + 1) * CONV_LANE_CHUNK)
            y_ref[:, lanes] = _conv_rows_by_residue(xp_ref, w_ref, r0, rows, lanes, CONV_HDR - p, CONV_WIDTH)
        y = y_ref[...] + bias_ref[...]
        mu = jnp.mean(y, axis=-1, keepdims=True)
        yc = y - mu
        var = jnp.mean(yc * yc, axis=-1, keepdims=True)
        z = yc * lax.rsqrt(var + EPS) * lng_ref[...] + lnb_ref[...]
        c_ref[pl.ds(r0, rows), :] = _silu(z).astype(BF16)
        return carry
    lax.fori_loop(0, tt // rows, body, 0)

    new_ref[0] = xp_ref[CONV_HDR + tt - p:CONV_HDR + tt, :]


def _conv_branch(proj, conv_prev, w, bias, lng, lnb, bsz, t_len):
    tt = min(t_len, 256)
    nt = t_len // tt
    p = CONV_WIDTH - 1
    blk_a = COL_GLU // CONV_DIM
    rows = min(tt, 32)
    return pl.pallas_call(
        functools.partial(_conv_branch_kernel, tt=tt, rows=rows),
        grid=(bsz, nt),
        in_specs=[pl.BlockSpec((tt, CONV_DIM), lambda b, t: (b * nt + t, blk_a)),
                  pl.BlockSpec((tt, CONV_DIM), lambda b, t: (b * nt + t, blk_a + 1)),
                  pl.BlockSpec((1, p, CONV_DIM), lambda b, t: (b, 0, 0)),
                  pl.BlockSpec((CONV_WIDTH, CONV_DIM), lambda b, t: (0, 0)),
                  pl.BlockSpec((1, CONV_DIM), lambda b, t: (0, 0)),
                  pl.BlockSpec((1, CONV_DIM), lambda b, t: (0, 0)),
                  pl.BlockSpec((1, CONV_DIM), lambda b, t: (0, 0))],
        out_specs=[pl.BlockSpec((tt, CONV_DIM), lambda b, t: (b * nt + t, 0)),
                   pl.BlockSpec((1, p, CONV_DIM), lambda b, t: (b, 0, 0))],
        out_shape=[jax.ShapeDtypeStruct((bsz * t_len, CONV_DIM), BF16),
                   jax.ShapeDtypeStruct((bsz, p, CONV_DIM), F32)],
        scratch_shapes=[pltpu.VMEM((CONV_HDR + tt, CONV_DIM), F32),
                        pltpu.VMEM((rows, CONV_DIM), F32)],
        compiler_params=_params("arbitrary", "arbitrary"),
        name="conv_branch",
    )(proj, proj, conv_prev, w, bias, lng, lnb)


DN_HDR = 8


def _dn_prep_kernel(x_ref, ba_ref, prev_ref, w_ref, alog_ref, dt_ref,
                    qkv_ref, bg_ref, new_ref, xp_ref, *, tt, rows):
    t = pl.program_id(1)
    p = DN_CONV_WIDTH - 1

    xp_ref[0:DN_HDR - p, :] = jnp.zeros((DN_HDR - p, DN_CONV_CH), F32)

    @pl.when(t == 0)
    def _():
        xp_ref[DN_HDR - p:DN_HDR, :] = prev_ref[0]

    @pl.when(t > 0)
    def _():
        xp_ref[DN_HDR - p:DN_HDR, :] = xp_ref[DN_HDR + tt - p:DN_HDR + tt, :]

    xp_ref[DN_HDR:DN_HDR + tt, :] = x_ref[...]

    def body(i, carry):
        r0 = pl.multiple_of(i * rows, rows)
        for hs in range(DN_CONV_CH // LANES):
            lanes = slice(hs * LANES, (hs + 1) * LANES)
            y = _silu(_conv_rows(xp_ref, w_ref, r0, rows, lanes, DN_HDR - p, DN_CONV_WIDTH))
            if hs < 2 * DN_HEADS:
                y = y * lax.rsqrt(jnp.sum(y * y, axis=-1, keepdims=True) + EPS)
                if hs < DN_HEADS:
                    y = y * (DN_DK ** -0.5)
            qkv_ref[pl.ds(r0, rows), lanes] = y
        return carry
    lax.fori_loop(0, tt // rows, body, 0)

    raw = ba_ref[...]
    sp_in = raw + dt_ref[...]
    softplus = jnp.maximum(sp_in, 0.0) + jnp.log(1.0 + jnp.exp(-jnp.abs(sp_in)))
    lane = lax.broadcasted_iota(jnp.int32, raw.shape, 1)
    bg_ref[...] = jnp.where(lane < DN_HEADS, _sigmoid(raw), -jnp.exp(alog_ref[...]) * softplus)

    new_ref[0] = xp_ref[DN_HDR + tt - p:DN_HDR + tt, :]


def _dn_prep(proj, dconv_prev, w, alog_pad, dt_pad, bsz, t_len):
    tt = min(t_len, 256)
    nt = t_len // tt
    p = DN_CONV_WIDTH - 1
    return pl.pallas_call(
        functools.partial(_dn_prep_kernel, tt=tt, rows=16),
        grid=(bsz, nt),
        in_specs=[pl.BlockSpec((tt, DN_CONV_CH), lambda b, t: (b * nt + t, COL_QKV // DN_CONV_CH)),
                  pl.BlockSpec((tt, LANES), lambda b, t: (b * nt + t, COL_BA // LANES)),
                  pl.BlockSpec((1, p, DN_CONV_CH), lambda b, t: (b, 0, 0)),
                  pl.BlockSpec((DN_CONV_WIDTH, DN_CONV_CH), lambda b, t: (0, 0)),
                  pl.BlockSpec((1, LANES), lambda b, t: (0, 0)),
                  pl.BlockSpec((1, LANES), lambda b, t: (0, 0))],
        out_specs=[pl.BlockSpec((tt, DN_CONV_CH), lambda b, t: (b * nt + t, 0)),
                   pl.BlockSpec((tt, LANES), lambda b, t: (b * nt + t, 0)),
                   pl.BlockSpec((1, p, DN_CONV_CH), lambda b, t: (b, 0, 0))],
        out_shape=[jax.ShapeDtypeStruct((bsz * t_len, DN_CONV_CH), F32),
                   jax.ShapeDtypeStruct((bsz * t_len, LANES), F32),
                   jax.ShapeDtypeStruct((bsz, p, DN_CONV_CH), F32)],
        scratch_shapes=[pltpu.VMEM((DN_HDR + tt, DN_CONV_CH), F32)],
        compiler_params=_params("arbitrary", "arbitrary"),
        name="dn_prep",
    )(proj, proj, dconv_prev, w, alog_pad, dt_pad)


def _dot_hi(a, b):
    return jnp.dot(a, b, precision=HI, preferred_element_type=F32)


_NN = (((1,), (0,)), ((), ()))
_NT = (((1,), (1,)), ((), ()))


def _split(a):
    hi = a.astype(BF16)
    lo = (a - hi.astype(F32)).astype(BF16)
    return hi, lo


def _mm3(a, b, dims=_NN, passes=3):
    a_hi, a_lo = a
    b_hi, b_lo = b
    out = lax.dot_general(a_hi, b_hi, dims, preferred_element_type=F32)
    if passes == 3:
        out = (out + lax.dot_general(a_hi, b_lo, dims, preferred_element_type=F32)
               + lax.dot_general(a_lo, b_hi, dims, preferred_element_type=F32))
    return out


PASSES_GRAM = 1
PASSES_INVERSE = 1
PASSES_STATE = 1


def _gated_delta_kernel(q_ref, k_ref, v_ref, bcol_ref, gcol_ref, grow_ref, s0_ref, o_ref, s_ref, *, chunk):
    c = pl.program_id(1)

    @pl.when(c == 0)
    def _():
        s_ref[...] = s0_ref[...]

    row = lax.broadcasted_iota(jnp.int32, (chunk, chunk), 0)
    col = lax.broadcasted_iota(jnp.int32, (chunk, chunk), 1)
    causal = row >= col
    strict = row > col
    lower = causal.astype(F32)
    upper = (row <= col).astype(F32)
    eye = (row == col).astype(F32)

    n_b = q_ref.shape[0]
    g_cum_col = [_dot_hi(lower, gcol_ref[b]) for b in range(n_b)]
    g_cum_row = [_dot_hi(grow_ref[b, 0], upper) for b in range(n_b)]
    n_sq = max(1, int(math.ceil(math.log2(chunk))) - 1)

    chains = [(b, hh) for b in range(n_b) for hh in range(DN_HEADS)]
    heads = range(len(chains))
    lanes = [slice(hh * DN_DK, (hh + 1) * DN_DK) for _, hh in chains]
    q = [q_ref[b, :, lanes[h]] for h, (b, _) in enumerate(chains)]
    k = [k_ref[b, :, lanes[h]] for h, (b, _) in enumerate(chains)]
    v = [v_ref[b, :, lanes[h]] for h, (b, _) in enumerate(chains)]
    beta = [bcol_ref[b, :, hh:hh + 1] for b, hh in chains]
    gc = [g_cum_col[b][:, hh:hh + 1] for b, hh in chains]
    gr = [g_cum_row[b][hh:hh + 1, :] for b, hh in chains]
    g_last = [gr[h][:, chunk - 1:chunk] for h in heads]
    k_sp = [_split(k[h]) for h in heads]
    q_sp = [_split(q[h]) for h in heads]
    s_old = [s_ref[b, hh] for b, hh in chains]
    s_sp = [_split(s_old[h]) for h in heads]
    decay = [jnp.where(causal, jnp.exp(jnp.where(causal, gc[h] - gr[h], 0.0)), 0.0) for h in heads]
    a_mat = [jnp.where(strict, beta[h] * _mm3(k_sp[h], k_sp[h], _NT, PASSES_GRAM) * decay[h], 0.0)
             for h in heads]
    qk = [_mm3(q_sp[h], k_sp[h], _NT, PASSES_GRAM) * decay[h] for h in heads]
    gam = [jnp.exp(gc[h]) for h in heads]
    qg_s = [_mm3(_split(q[h] * gam[h]), s_sp[h], _NN, PASSES_STATE) for h in heads]
    inv = [eye - a_mat[h] for h in heads]
    a_pow = [_split(a_mat[h]) for h in heads]
    for _ in range(n_sq):
        a_pow = [_split(_mm3(a_pow[h], a_pow[h], _NN, PASSES_INVERSE)) for h in heads]
        inv = [inv[h] + _mm3(_split(inv[h]), a_pow[h], _NN, PASSES_INVERSE) for h in heads]
    rhs = [jnp.concatenate([beta[h] * v[h], (beta[h] * gam[h]) * k[h]], axis=1) for h in heads]
    sol = [_mm3(_split(inv[h]), _split(rhs[h]), _NN, PASSES_INVERSE) for h in heads]
    delta = [sol[h][:, :DN_DV] - _mm3(_split(sol[h][:, DN_DV:]), s_sp[h], _NN, PASSES_STATE) for h in heads]
    d_sp = [_split(delta[h]) for h in heads]
    for h, (b, _) in enumerate(chains):
        o_ref[b, :, lanes[h]] = qg_s[h] + _mm3(_split(qk[h]), d_sp[h], _NN, PASSES_STATE)
    for h, (b, hh) in enumerate(chains):
        kd_t = (k[h] * jnp.exp(g_last[h] - gc[h])).T
        s_ref[b, hh] = jnp.exp(g_last[h]) * s_old[h] + _mm3(_split(kd_t), d_sp[h], _NN, PASSES_STATE)


DELTA_BATCH = 2


def _gated_delta(qkv, beta, g, s0, bsz, t_len, chunk):
    nc = t_len // chunk
    nb = DELTA_BATCH
    qkv3 = qkv.reshape(bsz, t_len, DN_CONV_CH)
    beta3 = beta.reshape(bsz, t_len, DN_HEADS)
    g3 = g.reshape(bsz, t_len, DN_HEADS)
    g_row = jnp.swapaxes(g.reshape(bsz, nc, chunk, DN_HEADS), 2, 3)
    o, s_new = pl.pallas_call(
        functools.partial(_gated_delta_kernel, chunk=chunk),
        grid=(bsz // nb, nc),
        in_specs=[pl.BlockSpec((nb, chunk, DN_KEY), lambda b, c: (b, c, 0)),
                  pl.BlockSpec((nb, chunk, DN_KEY), lambda b, c: (b, c, 1)),
                  pl.BlockSpec((nb, chunk, DN_VAL), lambda b, c: (b, c, 2)),
                  pl.BlockSpec((nb, chunk, DN_HEADS), lambda b, c: (b, c, 0)),
                  pl.BlockSpec((nb, chunk, DN_HEADS), lambda b, c: (b, c, 0)),
                  pl.BlockSpec((nb, 1, DN_HEADS, chunk), lambda b, c: (b, c, 0, 0)),
                  pl.BlockSpec((nb, DN_HEADS, DN_DK, DN_DV), lambda b, c: (b, 0, 0, 0))],
        out_specs=[pl.BlockSpec((nb, chunk, DN_VAL), lambda b, c: (b, c, 0)),
                   pl.BlockSpec((nb, DN_HEADS, DN_DK, DN_DV), lambda b, c: (b, 0, 0, 0))],
        out_shape=[jax.ShapeDtypeStruct((bsz, t_len, DN_VAL), F32),
                   jax.ShapeDtypeStruct((bsz, DN_HEADS, DN_DK, DN_DV), F32)],
        compiler_params=_params("arbitrary", "arbitrary"),
        name="gated_delta",
    )(qkv3, qkv3, qkv3, beta3, g3, g_row, s0)
    return o.reshape(bsz * t_len, DN_VAL), s_new


def _merge_kernel(c_ref, o_ref, z_ref, gc_ref, gd_ref, x_ref, ng_ref, pw2_ref, wo_ref, wout_ref,
                  x1_ref, og_ref):
    for h in range(DN_HEADS):
        lanes = slice(h * DN_DV, (h + 1) * DN_DV)
        oh = o_ref[:, lanes]
        oh = oh * lax.rsqrt(jnp.mean(oh * oh, axis=-1, keepdims=True) + EPS) * ng_ref[...]
        og_ref[:, lanes] = (oh * _silu(z_ref[:, lanes])).astype(BF16)
    y_dn = jnp.dot(og_ref[...], wo_ref[...], preferred_element_type=F32)
    y_conv = jnp.dot(c_ref[...], pw2_ref[...], preferred_element_type=F32)
    mix = _sigmoid(gc_ref[...]) * y_conv + _sigmoid(gd_ref[...]) * y_dn
    x1_ref[...] = x_ref[...] + jnp.dot(mix.astype(BF16), wout_ref[...], preferred_element_type=F32)


def _merge(c, o, proj, x2, ng, pw2, wo, wout):
    n = x2.shape[0]
    tm = min(n, 256)
    return pl.pallas_call(
        _merge_kernel,
        grid=(n // tm,),
        in_specs=[pl.BlockSpec((tm, CONV_DIM), lambda i: (i, 0)),
                  pl.BlockSpec((tm, DN_VAL), lambda i: (i, 0)),
                  pl.BlockSpec((tm, DN_VAL), lambda i: (i, COL_Z // DN_VAL)),
                  pl.BlockSpec((tm, D_MODEL), lambda i: (i, COL_GATE // D_MODEL)),
                  pl.BlockSpec((tm, D_MODEL), lambda i: (i, COL_GATE // D_MODEL + 1)),
                  pl.BlockSpec((tm, D_MODEL), lambda i: (i, 0)),
                  _const_spec((1, DN_DV)),
                  _const_spec((CONV_DIM, D_MODEL)),
                  _const_spec((DN_VAL, D_MODEL)),
                  _const_spec((D_MODEL, D_MODEL))],
        out_specs=pl.BlockSpec((tm, D_MODEL), lambda i: (i, 0)),
        out_shape=jax.ShapeDtypeStruct((n, D_MODEL), F32),
        scratch_shapes=[pltpu.VMEM((tm, DN_VAL), BF16)],
        compiler_params=_params("arbitrary"),
        name="merge",
    )(c, o, proj, proj, proj, x2, ng, pw2, wo, wout)


CAND_ROWS = 8 * 8 + 8 + 8


PEER_ET = 1024
PEER_KEY_BLOCK = SUBLANES
PEER_KEYS_PER_STEP = PEER_ET // N_KEYS
PEER_KEY_ROWS = N_KEYS // PEER_KEYS_PER_STEP * PEER_KEY_BLOCK


def _store_key_groups(dst_ref, h, lanes, x):
    for g in range(N_KEYS // PEER_KEYS_PER_STEP):
        dst_ref[h, g * PEER_KEY_BLOCK:g * PEER_KEY_BLOCK + PEER_KEYS_PER_STEP, lanes] = (
            x[g * PEER_KEYS_PER_STEP:(g + 1) * PEER_KEYS_PER_STEP, :])


def _sorting_network(n):
    size = 1
    while size < n:
        size *= 2
    pairs = []
    p = 1
    while p < size:
        k = p
        while k >= 1:
            for j in range(k % p, size - k, 2 * k):
                for i in range(min(k, size - j - k)):
                    if (i + j) // (2 * p) == (i + j + k) // (2 * p) and i + j + k < n:
                        pairs.append((i + j, i + j + k))
            k //= 2
        p *= 2
    return pairs


def _top_rows(xs, dsts, k):
    slabs = [[x[i:i + SUBLANES, :] for i in range(0, x.shape[0], SUBLANES)] for x in xs]
    for s in slabs:
        for i, j in _sorting_network(len(s)):
            s[i], s[j] = jnp.maximum(s[i], s[j]), jnp.minimum(s[i], s[j])
    for r in range(k):
        for c, (dst_ref, lanes) in enumerate(dsts):
            s = slabs[c]
            m = s[0]
            for shift in (4, 2, 1):
                m = jnp.maximum(m, pltpu.roll(m, shift, 0))
            dst_ref[r:r + 1, lanes] = m[0:1, :]
            popped = s[0] == m
            for d in range(min(len(s), k - 1 - r)):
                below = s[d + 1] if d + 1 < len(s) else -jnp.inf
                s[d] = jnp.where(popped, below, s[d])


def _peer_score_kernel(x1_ref, g_ref, wq_ref, keys_ref, xn_ref, s1_ref, a_ref, s2_ref, b_ref, tau_ref,
                       q_scr, c1_scr, c2_scr, cand_scr, top_scr, *, tn):
    x = x1_ref[...]
    xn = x * lax.rsqrt(jnp.mean(x * x, axis=-1, keepdims=True) + EPS) * g_ref[...]
    xn_ref[...] = xn.T.astype(BF16)
    q = jnp.dot(xn.astype(BF16), wq_ref[...], preferred_element_type=F32)
    for hp in range(2 * PEER_HEADS):
        q_scr[hp] = q[:, hp * PEER_HALF:(hp + 1) * PEER_HALF].astype(BF16)

    def head(h, carry):
        s1 = lax.dot_general(keys_ref[2 * h], q_scr[2 * h], (((1,), (1,)), ((), ())),
                             preferred_element_type=F32)
        s2 = lax.dot_general(keys_ref[2 * h + 1], q_scr[2 * h + 1], (((1,), (1,)), ((), ())),
                             preferred_element_type=F32)
        if PEER_KEYS_PER_STEP < PEER_KEY_BLOCK:
            s1_ref[h] = jnp.zeros(s1_ref.shape[1:], F32)
            a_ref[h] = jnp.zeros(a_ref.shape[1:], F32)
        _store_key_groups(s1_ref, h, slice(None), s1)
        s2_ref[h] = s2
        chunks = [slice(lc * LANES, (lc + 1) * LANES) for lc in range(tn // LANES)]
        for lanes in chunks:
            _top_rows([s1[:, lanes], s2[:, lanes]], [(c1_scr, lanes), (c2_scr, lanes)], PEER_TOPK)
        for lanes in chunks:
            c1 = c1_scr[:, lanes]
            c2 = c2_scr[:, lanes]
            for r1 in range(8):
                cand_scr[r1 * 8:(r1 + 1) * 8, lanes] = c1[r1:r1 + 1, :] + c2[0:8, :]
            cand_scr[64:72, lanes] = c1[8:16, :] + c2[0:1, :]
            cand_scr[72:80, lanes] = c1[0:1, :] + c2[8:16, :]
        _top_rows([cand_scr[:, lanes] for lanes in chunks], [(top_scr, lanes) for lanes in chunks], PEER_TOPK)
        for lanes in chunks:
            top = top_scr[:, lanes]
            m0 = top[0:1, :]
            zsum = jnp.sum(jnp.exp(top - m0), axis=0, keepdims=True)
            tau_ref[h, :, lanes] = top[PEER_TOPK - 1:PEER_TOPK, :]
            _store_key_groups(a_ref, h, lanes, jnp.exp(s1[:, lanes] - c1_scr[0:1, lanes]))
            b_ref[h, :, lanes] = jnp.exp(s2[:, lanes] - c2_scr[0:1, lanes]) / zsum
        return carry
    lax.fori_loop(0, PEER_HEADS, head, 0)


def _peer_score(x1, g, wq, keys):
    n = x1.shape[0]
    tn = min(n, 256)
    hk = (PEER_HEADS, N_KEYS, n)
    hk_spec = pl.BlockSpec((PEER_HEADS, N_KEYS, tn), lambda i: (0, 0, i))
    hi = (PEER_HEADS, PEER_KEY_ROWS, n)
    hi_spec = pl.BlockSpec((PEER_HEADS, PEER_KEY_ROWS, tn), lambda i: (0, 0, i))
    return pl.pallas_call(
        functools.partial(_peer_score_kernel, tn=tn),
        grid=(n // tn,),
        in_specs=[pl.BlockSpec((tn, D_MODEL), lambda i: (i, 0)),
                  _const_spec((1, D_MODEL)),
                  _const_spec((D_MODEL, 2 * PEER_HEADS * PEER_HALF)),
                  _const_spec((2 * PEER_HEADS, N_KEYS, PEER_HALF))],
        out_specs=[pl.BlockSpec((D_MODEL, tn), lambda i: (0, i)),
                   hi_spec, hi_spec, hk_spec, hk_spec,
                   pl.BlockSpec((PEER_HEADS, 1, tn), lambda i: (0, 0, i))],
        out_shape=[jax.ShapeDtypeStruct((D_MODEL, n), BF16),
                   jax.ShapeDtypeStruct(hi, F32), jax.ShapeDtypeStruct(hi, F32),
                   jax.ShapeDtypeStruct(hk, F32), jax.ShapeDtypeStruct(hk, F32),
                   jax.ShapeDtypeStruct((PEER_HEADS, 1, n), F32)],
        scratch_shapes=[pltpu.VMEM((2 * PEER_HEADS, tn, PEER_HALF), BF16),
                        pltpu.VMEM((PEER_TOPK, tn), F32),
                        pltpu.VMEM((PEER_TOPK, tn), F32),
                        pltpu.VMEM((CAND_ROWS, tn), F32),
                        pltpu.VMEM((PEER_TOPK, tn), F32)],
        compiler_params=_params("arbitrary"),
        name="peer_score",
    )(x1, g, wq, keys)


PEER_UP_PIECES = 2
PEER_DOWN_PIECES = 4
PEER_IG = 2
PEER_JB = 64

_GELU_C = math.sqrt(2.0 / math.pi)


def _gelu_tanh(x):
    return 0.5 * x * (1.0 + jnp.tanh(_GELU_C * (x + 0.044715 * (x * x * x))))


def _peer_dense_kernel(xn_ref, u_ref, vt_ref, s1_ref, a_ref, s2_ref, b_ref, tau_ref, o_ref,
                       act_scr, w_scr, hid_scr, *, tn, steps_per_tile):
    s = pl.program_id(0)
    prev = jnp.maximum(s - 1, 0)
    cur_slot = s % 2
    prev_slot = 1 - cur_slot

    @pl.when(s == 0)
    def _():
        act_scr[1] = jnp.zeros(act_scr.shape[1:], BF16)

    @pl.when(prev % steps_per_tile == 0)
    def _():
        o_ref[...] = jnp.zeros_like(o_ref)

    keys_per_step = PEER_ET // N_KEYS
    piece_w = min(tn, 2 * LANES)
    up_rows = PEER_ET // PEER_UP_PIECES
    down_rows = D_MODEL // PEER_DOWN_PIECES

    def up_proj(p, nq):
        rows, cols = slice(p * up_rows, (p + 1) * up_rows), slice(nq * piece_w, (nq + 1) * piece_w)
        hid_scr[rows, cols] = jnp.dot(u_ref[rows, :], xn_ref[:, cols], preferred_element_type=F32)

    def down_proj(p, nq):
        rows, cols = slice(p * down_rows, (p + 1) * down_rows), slice(nq * piece_w, (nq + 1) * piece_w)
        o_ref[rows, cols] += jnp.dot(vt_ref[rows, :], act_scr[prev_slot, :, cols],
                                     preferred_element_type=F32)

    def routing(ig, lc, jb):
        lanes = slice(lc * LANES, (lc + 1) * LANES)
        js = slice(jb * PEER_JB, (jb + 1) * PEER_JB)
        wgt = [None] * PEER_IG
        for h in range(PEER_HEADS):
            s2 = s2_ref[h, js, lanes]
            bb = b_ref[h, js, lanes]
            tau = tau_ref[h, :, lanes]
            for ii in range(PEER_IG):
                i_loc = ig * PEER_IG + ii
                sel = jnp.where(s1_ref[h, i_loc:i_loc + 1, lanes] + s2 >= tau, bb, 0.0)
                term = a_ref[h, i_loc:i_loc + 1, lanes] * sel
                wgt[ii] = term if wgt[ii] is None else wgt[ii] + term
        for ii in range(PEER_IG):
            r0 = (ig * PEER_IG + ii) * N_KEYS + jb * PEER_JB
            w_scr[r0:r0 + PEER_JB, lanes] = wgt[ii]

    def activate(p, nq):
        rows, cols = slice(p * up_rows, (p + 1) * up_rows), slice(nq * piece_w, (nq + 1) * piece_w)
        act_scr[cur_slot, rows, cols] = (_gelu_tanh(hid_scr[rows, cols]) * w_scr[rows, cols]).astype(BF16)

    n_q = tn // piece_w
    mxu_work = ([functools.partial(up_proj, p, nq) for p in range(PEER_UP_PIECES) for nq in range(n_q)]
                + [functools.partial(down_proj, p, nq) for p in range(PEER_DOWN_PIECES) for nq in range(n_q)])
    vpu_work = []
    groups_per_piece = keys_per_step // PEER_UP_PIECES // PEER_IG
    for p in range(PEER_UP_PIECES):
        for nq in range(n_q):
            for lc in range(nq * piece_w // LANES, (nq + 1) * piece_w // LANES):
                vpu_work += [functools.partial(routing, ig, lc, jb)
                             for ig in range(p * groups_per_piece, (p + 1) * groups_per_piece)
                             for jb in range(N_KEYS // PEER_JB)]
            vpu_work.append(functools.partial(activate, p, nq))
    emitted = 0
    for j, work in enumerate(vpu_work):
        while emitted * len(vpu_work) <= j * len(mxu_work) and emitted < len(mxu_work):
            mxu_work[emitted]()
            emitted += 1
        work()
    for work in mxu_work[emitted:]:
        work()


def _peer_dense(xn, u_bf, vt_bf, s1, a, s2, b, tau):
    n = xn.shape[1]
    tn = min(n, 512)
    spt = N_EXPERTS // PEER_ET
    total = (n // tn) * spt

    def cur(s):
        return jnp.minimum(s, total - 1)

    def prev(s):
        return jnp.maximum(s - 1, 0)

    hk_spec = pl.BlockSpec((PEER_HEADS, N_KEYS, tn), lambda s: (0, 0, cur(s) // spt))
    hi_spec = pl.BlockSpec((PEER_HEADS, PEER_KEY_BLOCK, tn), lambda s: (0, cur(s) % spt, cur(s) // spt))
    return pl.pallas_call(
        functools.partial(_peer_dense_kernel, tn=tn, steps_per_tile=spt),
        grid=(total + 1,),
        in_specs=[pl.BlockSpec((D_MODEL, tn), lambda s: (0, cur(s) // spt)),
                  pl.BlockSpec((PEER_ET, D_MODEL), lambda s: (cur(s) % spt, 0)),
                  pl.BlockSpec((D_MODEL, PEER_ET), lambda s: (0, prev(s) % spt)),
                  hi_spec, hi_spec, hk_spec, hk_spec,
                  pl.BlockSpec((PEER_HEADS, 1, tn), lambda s: (0, 0, cur(s) // spt))],
        out_specs=pl.BlockSpec((D_MODEL, tn), lambda s: (0, prev(s) // spt)),
        out_shape=jax.ShapeDtypeStruct((D_MODEL, n), F32),
        scratch_shapes=[pltpu.VMEM((2, PEER_ET, tn), BF16),
                        pltpu.VMEM((PEER_ET, tn), F32),
                        pltpu.VMEM((PEER_ET, tn), F32)],
        compiler_params=_params("arbitrary"),
        name="peer_dense",
    )(xn, u_bf, vt_bf, s1, a, s2, b, tau)


def _final_kernel(x1_ref, pt_ref, g_ref, y_ref):
    x = x1_ref[...] + pt_ref[...].T
    y_ref[...] = x * lax.rsqrt(jnp.mean(x * x, axis=-1, keepdims=True) + EPS) * g_ref[...]


def _final(x1, peer_t, g):
    n = x1.shape[0]
    tn = min(n, 256)
    return pl.pallas_call(
        _final_kernel,
        grid=(n // tn,),
        in_specs=[pl.BlockSpec((tn, D_MODEL), lambda i: (i, 0)),
                  pl.BlockSpec((D_MODEL, tn), lambda i: (0, i)),
                  pl.BlockSpec((1, D_MODEL), lambda i: (0, 0))],
        out_specs=pl.BlockSpec((tn, D_MODEL), lambda i: (i, 0)),
        out_shape=jax.ShapeDtypeStruct((n, D_MODEL), F32),
        compiler_params=_params("arbitrary"),
        name="final",
    )(x1, peer_t, g)


def _prep_weights(norm_mix_g, w_in, conv_dw_w, conv_dw_b, conv_ln_g, conv_ln_b, conv_pw2, dn_conv_w,
                  dn_a_log, dn_dt_bias, dn_norm_g, dn_w_o, w_out, norm_ffn_g, peer_w_q, peer_sub_keys,
                  peer_u, peer_v, norm_final_g):
    w = w_in[0]
    glu_end = 2 * CONV_DIM
    qkv_end = glu_end + DN_CONV_CH
    ba_end = qkv_end + 2 * DN_HEADS
    z_end = ba_end + DN_VAL
    w_pack = jnp.concatenate(
        [w[:, glu_end:qkv_end], w[:, :glu_end], w[:, ba_end:z_end], w[:, z_end:], w[:, qkv_end:ba_end],
         jnp.zeros((D_MODEL, LANES - 2 * DN_HEADS), w.dtype)], axis=1).astype(BF16)
    pad_lo = jnp.zeros((DN_HEADS,), F32)
    pad_hi = jnp.zeros((LANES - 2 * DN_HEADS,), F32)
    return dict(
        norm_mix_g=norm_mix_g[0][None, :], w_pack=w_pack,
        conv_dw_w=conv_dw_w[0], conv_dw_b=conv_dw_b[0][None, :],
        conv_ln_g=conv_ln_g[0][None, :], conv_ln_b=conv_ln_b[0][None, :],
        conv_pw2=conv_pw2[0].astype(BF16), dn_conv_w=dn_conv_w[0],
        alog_pad=jnp.concatenate([pad_lo, dn_a_log[0], pad_hi])[None, :],
        dt_pad=jnp.concatenate([pad_lo, dn_dt_bias[0], pad_hi])[None, :],
        dn_norm_g=dn_norm_g[0][None, :], dn_w_o=dn_w_o[0].astype(BF16), w_out=w_out[0].astype(BF16),
        norm_ffn_g=norm_ffn_g[0][None, :], peer_w_q=peer_w_q[0].astype(BF16),
        peer_keys=peer_sub_keys[0].reshape(2 * PEER_HEADS, N_KEYS, PEER_HALF).astype(BF16),
        peer_u=peer_u[0].astype(BF16), peer_vt=peer_v[0].T.astype(BF16),
        norm_final_g=norm_final_g[None, :])


def _trunk(x, conv_st, dconv_st, dn_st, wd, chunk):
    bsz, t_len, _ = x.shape
    n = bsz * t_len
    x2 = x.reshape(n, D_MODEL)
    proj = _in_proj(x2, wd["norm_mix_g"], wd["w_pack"])
    c_act, conv_new = _conv_branch(proj, conv_st, wd["conv_dw_w"], wd["conv_dw_b"], wd["conv_ln_g"],
                                   wd["conv_ln_b"], bsz, t_len)
    qkv, bg, dconv_new = _dn_prep(proj, dconv_st, wd["dn_conv_w"], wd["alog_pad"], wd["dt_pad"], bsz, t_len)
    beta = bg[:, :DN_HEADS]
    g = bg[:, DN_HEADS:2 * DN_HEADS]
    o, s_new = _gated_delta(qkv, beta, g, dn_st, bsz, t_len, chunk)
    x1 = _merge(c_act, o, proj, x2, wd["dn_norm_g"], wd["conv_pw2"], wd["dn_w_o"], wd["w_out"])
    xn2, s1, a, s2, b, tau = _peer_score(x1, wd["norm_ffn_g"], wd["peer_w_q"], wd["peer_keys"])
    peer_t = _peer_dense(xn2, wd["peer_u"], wd["peer_vt"], s1, a, s2, b, tau)
    y = _final(x1, peer_t, wd["norm_final_g"])
    return y.reshape(bsz, t_len, D_MODEL), conv_new[None], dconv_new[None], s_new[None]


def kernel(x_prompt, x_sample, state_conformer_conv, state_delta_conv, state_delta, norm_mix_g, w_in, conv_dw_w, conv_dw_b, conv_ln_g, conv_ln_b, conv_pw2, dn_conv_w, dn_a_log, dn_dt_bias, dn_norm_g, dn_w_o, w_out, norm_ffn_g, peer_w_q, peer_sub_keys, peer_u, peer_v, norm_final_g):
    wd = _prep_weights(norm_mix_g, w_in, conv_dw_w, conv_dw_b, conv_ln_g, conv_ln_b, conv_pw2, dn_conv_w,
                       dn_a_log, dn_dt_bias, dn_norm_g, dn_w_o, w_out, norm_ffn_g, peer_w_q, peer_sub_keys,
                       peer_u, peer_v, norm_final_g)
    bp = x_prompt.shape[0]
    zc = jnp.zeros((bp, CONV_WIDTH - 1, CONV_DIM), F32)
    zdc = jnp.zeros((bp, DN_CONV_WIDTH - 1, DN_CONV_CH), F32)
    zs = jnp.zeros((bp, DN_HEADS, DN_DK, DN_DV), F32)
    y_p, pc, pdc, ps = _trunk(x_prompt, zc, zdc, zs, wd, 64)
    y_s, sc, sdc, ss = _trunk(x_sample, state_conformer_conv[0], state_delta_conv[0], state_delta[0], wd,
                              x_sample.shape[1])
    return (y_p, y_s, pc, pdc, ps, sc, sdc, ss)
```

```python
import functools
import math

import jax
import jax.numpy as jnp
from jax import lax
from jax.experimental import pallas as pl
from jax.experimental.pallas import tpu as pltpu

F32 = jnp.float32
BF16 = jnp.bfloat16
HI = lax.Precision.HIGHEST

EPS = 1e-6
D_MODEL = 2048
CONV_DIM = 1024
CONV_WIDTH = 31
DN_HEADS = 8
DN_DK = 128
DN_DV = 128
DN_KEY = DN_HEADS * DN_DK
DN_VAL = DN_HEADS * DN_DV
DN_CONV_WIDTH = 4
DN_CONV_CH = 2 * DN_KEY + DN_VAL
PEER_HEADS = 8
PEER_HALF = 128
N_KEYS = 128
N_EXPERTS = N_KEYS * N_KEYS
PEER_TOPK = 16

LANES = 128
SUBLANES = 8
VMEM_LIMIT = 56 * 1024 * 1024

COL_QKV = 0
COL_GLU = COL_QKV + DN_CONV_CH
COL_Z = COL_GLU + 2 * CONV_DIM
COL_GATE = COL_Z + DN_VAL
COL_BA = COL_GATE + 2 * D_MODEL
PROJ_COLS = COL_BA + LANES


def _params(*sem, flags=None):
    return pltpu.CompilerParams(dimension_semantics=sem, vmem_limit_bytes=VMEM_LIMIT, flags=flags)


def _const_spec(shape):
    nd = len(shape)
    return pl.BlockSpec(shape, lambda *_: (0,) * nd, pipeline_mode=pl.Buffered(1))


def _sigmoid(x):
    return 1.0 / (1.0 + jnp.exp(-x))


def _silu(x):
    return x * _sigmoid(x)


def _in_proj_kernel(x_ref, g_ref, w_ref, o_ref, xn_ref, *, rows):
    @pl.when(pl.program_id(1) == 0)
    def _():
        def body(i, carry):
            r = pl.ds(pl.multiple_of(i * rows, rows), rows)
            x = x_ref[r, :]
            y = x * lax.rsqrt(jnp.mean(x * x, axis=-1, keepdims=True) + EPS)
            xn_ref[r, :] = (y * g_ref[...]).astype(BF16)
            return carry
        lax.fori_loop(0, x_ref.shape[0] // rows, body, 0)

    o_ref[...] = jnp.dot(xn_ref[...], w_ref[...], preferred_element_type=F32)


def _in_proj(x2, g, w_pack):
    n = x2.shape[0]
    tm = min(n, 1024)
    tn = 1152
    return pl.pallas_call(
        functools.partial(_in_proj_kernel, rows=min(tm, 64)),
        grid=(n // tm, PROJ_COLS // tn),
        in_specs=[pl.BlockSpec((tm, D_MODEL), lambda i, j: (i, 0)),
                  pl.BlockSpec((1, D_MODEL), lambda i, j: (0, 0)),
                  pl.BlockSpec((D_MODEL, tn), lambda i, j: (0, j))],
        out_specs=pl.BlockSpec((tm, tn), lambda i, j: (i, j)),
        out_shape=jax.ShapeDtypeStruct((n, PROJ_COLS), F32),
        scratch_shapes=[pltpu.VMEM((tm, D_MODEL), BF16)],
        compiler_params=_params("arbitrary", "arbitrary"),
        name="in_proj",
    )(x2, g, w_pack)


def _conv_rows(xp_ref, w_ref, r0, rows, lanes, first, kw):
    acc = None
    k = 0
    while k < kw:
        off = first + k
        base = (off // SUBLANES) * SUBLANES
        nk = min(kw - k, base + SUBLANES - off)
        extra = SUBLANES if off + nk - 1 > base else 0
        win = xp_ref[pl.ds(pl.multiple_of(r0 + base, SUBLANES), rows + extra), lanes]
        for kk in range(k, k + nk):
            s = first + kk - base
            term = w_ref[kk:kk + 1, lanes] * win[s:s + rows, :]
            acc = term if acc is None else acc + term
        k += nk
    return acc


def _conv_rows_by_residue(xp_ref, w_ref, r0, rows, lanes, first, kw):
    y = None
    for b in range(SUBLANES):
        taps = [k for k in range(kw) if (first + k) % SUBLANES == b]
        if not taps:
            continue
        span = rows + (SUBLANES if b else 0)
        z = None
        for k in taps:
            start = pl.multiple_of(r0 + (first + k) // SUBLANES * SUBLANES, SUBLANES)
            term = w_ref[k:k + 1, lanes] * xp_ref[pl.ds(start, span), lanes]
            z = term if z is None else z + term
        z = z[b:b + rows, :]
        y = z if y is None else y + z
    return y


CONV_HDR = 32
CONV_LANE_CHUNK = 256


def _conv_branch_kernel(a_ref, b_ref, prev_ref, w_ref, bias_ref, lng_ref, lnb_ref,
                        c_ref, new_ref, xp_ref, y_ref, *, tt, rows):
    t = pl.program_id(1)
    p = CONV_WIDTH - 1

    xp_ref[0:CONV_HDR - p, :] = jnp.zeros((CONV_HDR - p, CONV_DIM), F32)

    @pl.when(t == 0)
    def _():
        xp_ref[CONV_HDR - p:CONV_HDR, :] = prev_ref[0]

    @pl.when(t > 0)
    def _():
        xp_ref[CONV_HDR - p:CONV_HDR, :] = xp_ref[CONV_HDR + tt - p:CONV_HDR + tt, :]

    xp_ref[CONV_HDR:CONV_HDR + tt, :] = a_ref[...] * _sigmoid(b_ref[...])

    def body(i, carry):
        r0 = pl.multiple_of(i * rows, rows)
        for lc in range(CONV_DIM // CONV_LANE_CHUNK):
            lanes = slice(lc * CONV_LANE_CHUNK, (lc + 1) * CONV_LANE_CHUNK)
            y_ref[:, lanes] = _conv_rows_by_residue(xp_ref, w_ref, r0, rows, lanes, CONV_HDR - p, CONV_WIDTH)
        y = y_ref[...] + bias_ref[...]
        mu = jnp.mean(y, axis=-1, keepdims=True)
        yc = y - mu
        var = jnp.mean(yc * yc, axis=-1, keepdims=True)
        z = yc * lax.rsqrt(var + EPS) * lng_ref[...] + lnb_ref[...]
        c_ref[pl.ds(r0, rows), :] = _silu(z).astype(BF16)
        return carry
    lax.fori_loop(0, tt // rows, body, 0)

    new_ref[0] = xp_ref[CONV_HDR + tt - p:CONV_HDR + tt, :]


def _conv_branch(proj, conv_prev, w, bias, lng, lnb, bsz, t_len):
    tt = min(t_len, 256)
    nt = t_len // tt
    p = CONV_WIDTH - 1
    blk_a = COL_GLU // CONV_DIM
    rows = min(tt, 32)
    return pl.pallas_call(
        functools.partial(_conv_branch_kernel, tt=tt, rows=rows),
        grid=(bsz, nt),
        in_specs=[pl.BlockSpec((tt, CONV_DIM), lambda b, t: (b * nt + t, blk_a)),
                  pl.BlockSpec((tt, CONV_DIM), lambda b, t: (b * nt + t, blk_a + 1)),
                  pl.BlockSpec((1, p, CONV_DIM), lambda b, t: (b, 0, 0)),
                  pl.BlockSpec((CONV_WIDTH, CONV_DIM), lambda b, t: (0, 0)),
                  pl.BlockSpec((1, CONV_DIM), lambda b, t: (0, 0)),
                  pl.BlockSpec((1, CONV_DIM), lambda b, t: (0, 0)),
                  pl.BlockSpec((1, CONV_DIM), lambda b, t: (0, 0))],
        out_specs=[pl.BlockSpec((tt, CONV_DIM), lambda b, t: (b * nt + t, 0)),
                   pl.BlockSpec((1, p, CONV_DIM), lambda b, t: (b, 0, 0))],
        out_shape=[jax.ShapeDtypeStruct((bsz * t_len, CONV_DIM), BF16),
                   jax.ShapeDtypeStruct((bsz, p, CONV_DIM), F32)],
        scratch_shapes=[pltpu.VMEM((CONV_HDR + tt, CONV_DIM), F32),
                        pltpu.VMEM((rows, CONV_DIM), F32)],
        compiler_params=_params("arbitrary", "arbitrary"),
        name="conv_branch",
    )(proj, proj, conv_prev, w, bias, lng, lnb)


DN_HDR = 8


def _dn_prep_kernel(x_ref, ba_ref, prev_ref, w_ref, alog_ref, dt_ref,
                    qkv_ref, bg_ref, new_ref, xp_ref, *, tt, rows):
    t = pl.program_id(1)
    p = DN_CONV_WIDTH - 1

    xp_ref[0:DN_HDR - p, :] = jnp.zeros((DN_HDR - p, DN_CONV_CH), F32)

    @pl.when(t == 0)
    def _():
        xp_ref[DN_HDR - p:DN_HDR, :] = prev_ref[0]

    @pl.when(t > 0)
    def _():
        xp_ref[DN_HDR - p:DN_HDR, :] = xp_ref[DN_HDR + tt - p:DN_HDR + tt, :]

    xp_ref[DN_HDR:DN_HDR + tt, :] = x_ref[...]

    def body(i, carry):
        r0 = pl.multiple_of(i * rows, rows)
        for hs in range(DN_CONV_CH // LANES):
            lanes = slice(hs * LANES, (hs + 1) * LANES)
            y = _silu(_conv_rows(xp_ref, w_ref, r0, rows, lanes, DN_HDR - p, DN_CONV_WIDTH))
            if hs < 2 * DN_HEADS:
                y = y * lax.rsqrt(jnp.sum(y * y, axis=-1, keepdims=True) + EPS)
                if hs < DN_HEADS:
                    y = y * (DN_DK ** -0.5)
            qkv_ref[pl.ds(r0, rows), lanes] = y
        return carry
    lax.fori_loop(0, tt // rows, body, 0)

    raw = ba_ref[...]
    sp_in = raw + dt_ref[...]
    softplus = jnp.maximum(sp_in, 0.0) + jnp.log(1.0 + jnp.exp(-jnp.abs(sp_in)))
    lane = lax.broadcasted_iota(jnp.int32, raw.shape, 1)
    bg_ref[...] = jnp.where(lane < DN_HEADS, _sigmoid(raw), -jnp.exp(alog_ref[...]) * softplus)

    new_ref[0] = xp_ref[DN_HDR + tt - p:DN_HDR + tt, :]


def _dn_prep(proj, dconv_prev, w, alog_pad, dt_pad, bsz, t_len):
    tt = min(t_len, 256)
    nt = t_len // tt
    p = DN_CONV_WIDTH - 1
    return pl.pallas_call(
        functools.partial(_dn_prep_kernel, tt=tt, rows=16),
        grid=(bsz, nt),
        in_specs=[pl.BlockSpec((tt, DN_CONV_CH), lambda b, t: (b * nt + t, COL_QKV // DN_CONV_CH)),
                  pl.BlockSpec((tt, LANES), lambda b, t: (b * nt + t, COL_BA // LANES)),
                  pl.BlockSpec((1, p, DN_CONV_CH), lambda b, t: (b, 0, 0)),
                  pl.BlockSpec((DN_CONV_WIDTH, DN_CONV_CH), lambda b, t: (0, 0)),
                  pl.BlockSpec((1, LANES), lambda b, t: (0, 0)),
                  pl.BlockSpec((1, LANES), lambda b, t: (0, 0))],
        out_specs=[pl.BlockSpec((tt, DN_CONV_CH), lambda b, t: (b * nt + t, 0)),
                   pl.BlockSpec((tt, LANES), lambda b, t: (b * nt + t, 0)),
                   pl.BlockSpec((1, p, DN_CONV_CH), lambda b, t: (b, 0, 0))],
        out_shape=[jax.ShapeDtypeStruct((bsz * t_len, DN_CONV_CH), F32),
                   jax.ShapeDtypeStruct((bsz * t_len, LANES), F32),
                   jax.ShapeDtypeStruct((bsz, p, DN_CONV_CH), F32)],
        scratch_shapes=[pltpu.VMEM((DN_HDR + tt, DN_CONV_CH), F32)],
        compiler_params=_params("arbitrary", "arbitrary"),
        name="dn_prep",
    )(proj, proj, dconv_prev, w, alog_pad, dt_pad)


def _dot_hi(a, b):
    return jnp.dot(a, b, precision=HI, preferred_element_type=F32)


_NN = (((1,), (0,)), ((), ()))
_NT = (((1,), (1,)), ((), ()))


def _split(a):
    hi = a.astype(BF16)
    lo = (a - hi.astype(F32)).astype(BF16)
    return hi, lo


def _mm3(a, b, dims=_NN, passes=3):
    a_hi, a_lo = a
    b_hi, b_lo = b
    out = lax.dot_general(a_hi, b_hi, dims, preferred_element_type=F32)
    if passes == 3:
        out = (out + lax.dot_general(a_hi, b_lo, dims, preferred_element_type=F32)
               + lax.dot_general(a_lo, b_hi, dims, preferred_element_type=F32))
    return out


PASSES_GRAM = 1
PASSES_INVERSE = 1
PASSES_STATE = 1


def _gated_delta_kernel(q_ref, k_ref, v_ref, bcol_ref, gcol_ref, grow_ref, s0_ref, o_ref, s_ref, *, chunk):
    c = pl.program_id(1)

    @pl.when(c == 0)
    def _():
        s_ref[...] = s0_ref[...]

    row = lax.broadcasted_iota(jnp.int32, (chunk, chunk), 0)
    col = lax.broadcasted_iota(jnp.int32, (chunk, chunk), 1)
    causal = row >= col
    strict = row > col
    lower = causal.astype(F32)
    upper = (row <= col).astype(F32)
    eye = (row == col).astype(F32)

    n_b = q_ref.shape[0]
    g_cum_col = [_dot_hi(lower, gcol_ref[b]) for b in range(n_b)]
    g_cum_row = [_dot_hi(grow_ref[b, 0], upper) for b in range(n_b)]
    n_sq = max(1, int(math.ceil(math.log2(chunk))) - 1)

    chains = [(b, hh) for b in range(n_b) for hh in range(DN_HEADS)]
    heads = range(len(chains))
    lanes = [slice(hh * DN_DK, (hh + 1) * DN_DK) for _, hh in chains]
    q = [q_ref[b, :, lanes[h]] for h, (b, _) in enumerate(chains)]
    k = [k_ref[b, :, lanes[h]] for h, (b, _) in enumerate(chains)]
    v = [v_ref[b, :, lanes[h]] for h, (b, _) in enumerate(chains)]
    beta = [bcol_ref[b, :, hh:hh + 1] for b, hh in chains]
    gc = [g_cum_col[b][:, hh:hh + 1] for b, hh in chains]
    gr = [g_cum_row[b][hh:hh + 1, :] for b, hh in chains]
    g_last = [gr[h][:, chunk - 1:chunk] for h in heads]
    k_sp = [_split(k[h]) for h in heads]
    q_sp = [_split(q[h]) for h in heads]
    s_old = [s_ref[b, hh] for b, hh in chains]
    s_sp = [_split(s_old[h]) for h in heads]
    decay = [jnp.where(causal, jnp.exp(jnp.where(causal, gc[h] - gr[h], 0.0)), 0.0) for h in heads]
    a_mat = [jnp.where(strict, beta[h] * _mm3(k_sp[h], k_sp[h], _NT, PASSES_GRAM) * decay[h], 0.0)
             for h in heads]
    qk = [_mm3(q_sp[h], k_sp[h], _NT, PASSES_GRAM) * decay[h] for h in heads]
    gam = [jnp.exp(gc[h]) for h in heads]
    qg_s = [_mm3(_split(q[h] * gam[h]), s_sp[h], _NN, PASSES_STATE) for h in heads]
    inv = [eye - a_mat[h] for h in heads]
    a_pow = [_split(a_mat[h]) for h in heads]
    for _ in range(n_sq):
        a_pow = [_split(_mm3(a_pow[h], a_pow[h], _NN, PASSES_INVERSE)) for h in heads]
        inv = [inv[h] + _mm3(_split(inv[h]), a_pow[h], _NN, PASSES_INVERSE) for h in heads]
    rhs = [jnp.concatenate([beta[h] * v[h], (beta[h] * gam[h]) * k[h]], axis=1) for h in heads]
    sol = [_mm3(_split(inv[h]), _split(rhs[h]), _NN, PASSES_INVERSE) for h in heads]
    delta = [sol[h][:, :DN_DV] - _mm3(_split(sol[h][:, DN_DV:]), s_sp[h], _NN, PASSES_STATE) for h in heads]
    d_sp = [_split(delta[h]) for h in heads]
    for h, (b, _) in enumerate(chains):
        o_ref[b, :, lanes[h]] = qg_s[h] + _mm3(_split(qk[h]), d_sp[h], _NN, PASSES_STATE)
    for h, (b, hh) in enumerate(chains):
        kd_t = (k[h] * jnp.exp(g_last[h] - gc[h])).T
        s_ref[b, hh] = jnp.exp(g_last[h]) * s_old[h] + _mm3(_split(kd_t), d_sp[h], _NN, PASSES_STATE)


DELTA_BATCH = 2


def _gated_delta(qkv, beta, g, s0, bsz, t_len, chunk):
    nc = t_len // chunk
    nb = DELTA_BATCH
    qkv3 = qkv.reshape(bsz, t_len, DN_CONV_CH)
    beta3 = beta.reshape(bsz, t_len, DN_HEADS)
    g3 = g.reshape(bsz, t_len, DN_HEADS)
    g_row = jnp.swapaxes(g.reshape(bsz, nc, chunk, DN_HEADS), 2, 3)
    o, s_new = pl.pallas_call(
        functools.partial(_gated_delta_kernel, chunk=chunk),
        grid=(bsz // nb, nc),
        in_specs=[pl.BlockSpec((nb, chunk, DN_KEY), lambda b, c: (b, c, 0)),
                  pl.BlockSpec((nb, chunk, DN_KEY), lambda b, c: (b, c, 1)),
                  pl.BlockSpec((nb, chunk, DN_VAL), lambda b, c: (b, c, 2)),
                  pl.BlockSpec((nb, chunk, DN_HEADS), lambda b, c: (b, c, 0)),
                  pl.BlockSpec((nb, chunk, DN_HEADS), lambda b, c: (b, c, 0)),
                  pl.BlockSpec((nb, 1, DN_HEADS, chunk), lambda b, c: (b, c, 0, 0)),
                  pl.BlockSpec((nb, DN_HEADS, DN_DK, DN_DV), lambda b, c: (b, 0, 0, 0))],
        out_specs=[pl.BlockSpec((nb, chunk, DN_VAL), lambda b, c: (b, c, 0)),
                   pl.BlockSpec((nb, DN_HEADS, DN_DK, DN_DV), lambda b, c: (b, 0, 0, 0))],
        out_shape=[jax.ShapeDtypeStruct((bsz, t_len, DN_VAL), F32),
                   jax.ShapeDtypeStruct((bsz, DN_HEADS, DN_DK, DN_DV), F32)],
        compiler_params=_params("arbitrary", "arbitrary"),
        name="gated_delta",
    )(qkv3, qkv3, qkv3, beta3, g3, g_row, s0)
    return o.reshape(bsz * t_len, DN_VAL), s_new


def _merge_kernel(c_ref, o_ref, z_ref, gc_ref, gd_ref, x_ref, ng_ref, pw2_ref, wo_ref, wout_ref,
                  x1_ref, og_ref):
    for h in range(DN_HEADS):
        lanes = slice(h * DN_DV, (h + 1) * DN_DV)
        oh = o_ref[:, lanes]
        oh = oh * lax.rsqrt(jnp.mean(oh * oh, axis=-1, keepdims=True) + EPS) * ng_ref[...]
        og_ref[:, lanes] = (oh * _silu(z_ref[:, lanes])).astype(BF16)
    y_dn = jnp.dot(og_ref[...], wo_ref[...], preferred_element_type=F32)
    y_conv = jnp.dot(c_ref[...], pw2_ref[...], preferred_element_type=F32)
    mix = _sigmoid(gc_ref[...]) * y_conv + _sigmoid(gd_ref[...]) * y_dn
    x1_ref[...] = x_ref[...] + jnp.dot(mix.astype(BF16), wout_ref[...], preferred_element_type=F32)


def _merge(c, o, proj, x2, ng, pw2, wo, wout):
    n = x2.shape[0]
    tm = min(n, 256)
    return pl.pallas_call(
        _merge_kernel,
        grid=(n // tm,),
        in_specs=[pl.BlockSpec((tm, CONV_DIM), lambda i: (i, 0)),
                  pl.BlockSpec((tm, DN_VAL), lambda i: (i, 0)),
                  pl.BlockSpec((tm, DN_VAL), lambda i: (i, COL_Z // DN_VAL)),
                  pl.BlockSpec((tm, D_MODEL), lambda i: (i, COL_GATE // D_MODEL)),
                  pl.BlockSpec((tm, D_MODEL), lambda i: (i, COL_GATE // D_MODEL + 1)),
                  pl.BlockSpec((tm, D_MODEL), lambda i: (i, 0)),
                  _const_spec((1, DN_DV)),
                  _const_spec((CONV_DIM, D_MODEL)),
                  _const_spec((DN_VAL, D_MODEL)),
                  _const_spec((D_MODEL, D_MODEL))],
        out_specs=pl.BlockSpec((tm, D_MODEL), lambda i: (i, 0)),
        out_shape=jax.ShapeDtypeStruct((n, D_MODEL), F32),
        scratch_shapes=[pltpu.VMEM((tm, DN_VAL), BF16)],
        compiler_params=_params("arbitrary"),
        name="merge",
    )(c, o, proj, proj, proj, x2, ng, pw2, wo, wout)


CAND_ROWS = 8 * 8 + 8 + 8


PEER_ET = 1024
PEER_KEY_BLOCK = SUBLANES
PEER_KEYS_PER_STEP = PEER_ET // N_KEYS
PEER_KEY_ROWS = N_KEYS // PEER_KEYS_PER_STEP * PEER_KEY_BLOCK


def _store_key_groups(dst_ref, h, lanes, x):
    for g in range(N_KEYS // PEER_KEYS_PER_STEP):
        dst_ref[h, g * PEER_KEY_BLOCK:g * PEER_KEY_BLOCK + PEER_KEYS_PER_STEP, lanes] = (
            x[g * PEER_KEYS_PER_STEP:(g + 1) * PEER_KEYS_PER_STEP, :])


def _sorting_network(n):
    size = 1
    while size < n:
        size *= 2
    pairs = []
    p = 1
    while p < size:
        k = p
        while k >= 1:
            for j in range(k % p, size - k, 2 * k):
                for i in range(min(k, size - j - k)):
                    if (i + j) // (2 * p) == (i + j + k) // (2 * p) and i + j + k < n:
                        pairs.append((i + j, i + j + k))
            k //= 2
        p *= 2
    return pairs


def _top_rows(xs, dsts, k):
    slabs = [[x[i:i + SUBLANES, :] for i in range(0, x.shape[0], SUBLANES)] for x in xs]
    for s in slabs:
        for i, j in _sorting_network(len(s)):
            s[i], s[j] = jnp.maximum(s[i], s[j]), jnp.minimum(s[i], s[j])
    for r in range(k):
        for c, (dst_ref, lanes) in enumerate(dsts):
            s = slabs[c]
            m = s[0]
            for shift in (4, 2, 1):
                m = jnp.maximum(m, pltpu.roll(m, shift, 0))
            dst_ref[r:r + 1, lanes] = m[0:1, :]
            popped = s[0] == m
            for d in range(min(len(s), k - 1 - r)):
                below = s[d + 1] if d + 1 < len(s) else -jnp.inf
                s[d] = jnp.where(popped, below, s[d])


PEER_HEADS_PER_ITER = 4


def _peer_score_kernel(x1_ref, g_ref, wq_ref, keys_ref, xn_ref, s1_ref, a_ref, s2_ref, b_ref, tau_ref,
                       q_scr, c1_all, c2_all, cand_all, top_all, *, tn):
    x = x1_ref[...]
    xn = x * lax.rsqrt(jnp.mean(x * x, axis=-1, keepdims=True) + EPS) * g_ref[...]
    xn_ref[...] = xn.T.astype(BF16)
    q = jnp.dot(xn.astype(BF16), wq_ref[...], preferred_element_type=F32)
    for hp in range(2 * PEER_HEADS):
        q_scr[hp] = q[:, hp * PEER_HALF:(hp + 1) * PEER_HALF].astype(BF16)

    def head(h, slot):
        c1_scr, c2_scr, cand_scr, top_scr = (c1_all.at[slot], c2_all.at[slot], cand_all.at[slot],
                                             top_all.at[slot])
        s1 = lax.dot_general(keys_ref[2 * h], q_scr[2 * h], (((1,), (1,)), ((), ())),
                             preferred_element_type=F32)
        s2 = lax.dot_general(keys_ref[2 * h + 1], q_scr[2 * h + 1], (((1,), (1,)), ((), ())),
                             preferred_element_type=F32)
        if PEER_KEYS_PER_STEP < PEER_KEY_BLOCK:
            s1_ref[h] = jnp.zeros(s1_ref.shape[1:], F32)
            a_ref[h] = jnp.zeros(a_ref.shape[1:], F32)
        _store_key_groups(s1_ref, h, slice(None), s1)
        s2_ref[h] = s2
        chunks = [slice(lc * LANES, (lc + 1) * LANES) for lc in range(tn // LANES)]
        for lanes in chunks:
            _top_rows([s1[:, lanes], s2[:, lanes]], [(c1_scr, lanes), (c2_scr, lanes)], PEER_TOPK)
        for lanes in chunks:
            c1 = c1_scr[:, lanes]
            c2 = c2_scr[:, lanes]
            for r1 in range(8):
                cand_scr[r1 * 8:(r1 + 1) * 8, lanes] = c1[r1:r1 + 1, :] + c2[0:8, :]
            cand_scr[64:72, lanes] = c1[8:16, :] + c2[0:1, :]
            cand_scr[72:80, lanes] = c1[0:1, :] + c2[8:16, :]
        _top_rows([cand_scr[:, lanes] for lanes in chunks], [(top_scr, lanes) for lanes in chunks], PEER_TOPK)
        for lanes in chunks:
            top = top_scr[:, lanes]
            m0 = top[0:1, :]
            zsum = jnp.sum(jnp.exp(top - m0), axis=0, keepdims=True)
            tau_ref[h, :, lanes] = top[PEER_TOPK - 1:PEER_TOPK, :]
            _store_key_groups(a_ref, h, lanes, jnp.exp(s1[:, lanes] - c1_scr[0:1, lanes]))
            b_ref[h, :, lanes] = jnp.exp(s2[:, lanes] - c2_scr[0:1, lanes]) / zsum

    def head_group(i, carry):
        for slot in range(PEER_HEADS_PER_ITER):
            head(i * PEER_HEADS_PER_ITER + slot, slot)
        return carry
    lax.fori_loop(0, PEER_HEADS // PEER_HEADS_PER_ITER, head_group, 0)


def _peer_score(x1, g, wq, keys):
    n = x1.shape[0]
    tn = min(n, 256)
    hk = (PEER_HEADS, N_KEYS, n)
    hk_spec = pl.BlockSpec((PEER_HEADS, N_KEYS, tn), lambda i: (0, 0, i))
    hi = (PEER_HEADS, PEER_KEY_ROWS, n)
    hi_spec = pl.BlockSpec((PEER_HEADS, PEER_KEY_ROWS, tn), lambda i: (0, 0, i))
    return pl.pallas_call(
        functools.partial(_peer_score_kernel, tn=tn),
        grid=(n // tn,),
        in_specs=[pl.BlockSpec((tn, D_MODEL), lambda i: (i, 0)),
                  _const_spec((1, D_MODEL)),
                  _const_spec((D_MODEL, 2 * PEER_HEADS * PEER_HALF)),
                  _const_spec((2 * PEER_HEADS, N_KEYS, PEER_HALF))],
        out_specs=[pl.BlockSpec((D_MODEL, tn), lambda i: (0, i)),
                   hi_spec, hi_spec, hk_spec, hk_spec,
                   pl.BlockSpec((PEER_HEADS, 1, tn), lambda i: (0, 0, i))],
        out_shape=[jax.ShapeDtypeStruct((D_MODEL, n), BF16),
                   jax.ShapeDtypeStruct(hi, F32), jax.ShapeDtypeStruct(hi, F32),
                   jax.ShapeDtypeStruct(hk, F32), jax.ShapeDtypeStruct(hk, F32),
                   jax.ShapeDtypeStruct((PEER_HEADS, 1, n), F32)],
        scratch_shapes=[pltpu.VMEM((2 * PEER_HEADS, tn, PEER_HALF), BF16),
                        pltpu.VMEM((PEER_HEADS_PER_ITER, PEER_TOPK, tn), F32),
                        pltpu.VMEM((PEER_HEADS_PER_ITER, PEER_TOPK, tn), F32),
                        pltpu.VMEM((PEER_HEADS_PER_ITER, CAND_ROWS, tn), F32),
                        pltpu.VMEM((PEER_HEADS_PER_ITER, PEER_TOPK, tn), F32)],
        compiler_params=_params("arbitrary"),
        name="peer_score",
    )(x1, g, wq, keys)


PEER_UP_PIECES = 2
PEER_DOWN_PIECES = 4
PEER_IG = 2
PEER_JB = 64

_GELU_C = math.sqrt(2.0 / math.pi)


def _gelu_tanh(x):
    return 0.5 * x * (1.0 + jnp.tanh(_GELU_C * (x + 0.044715 * (x * x * x))))


def _peer_dense_kernel(xn_ref, u_ref, vt_ref, s1_ref, a_ref, s2_ref, b_ref, tau_ref, o_ref,
                       act_scr, w_scr, hid_scr, *, tn, steps_per_tile):
    s = pl.program_id(0)
    prev = jnp.maximum(s - 1, 0)
    cur_slot = s % 2
    prev_slot = 1 - cur_slot

    @pl.when(s == 0)
    def _():
        act_scr[1] = jnp.zeros(act_scr.shape[1:], BF16)

    @pl.when(prev % steps_per_tile == 0)
    def _():
        o_ref[...] = jnp.zeros_like(o_ref)

    keys_per_step = PEER_ET // N_KEYS
    piece_w = min(tn, 2 * LANES)
    up_rows = PEER_ET // PEER_UP_PIECES
    down_rows = D_MODEL // PEER_DOWN_PIECES

    def up_proj(p, nq):
        rows, cols = slice(p * up_rows, (p + 1) * up_rows), slice(nq * piece_w, (nq + 1) * piece_w)
        hid_scr[rows, cols] = jnp.dot(u_ref[rows, :], xn_ref[:, cols], preferred_element_type=F32)

    def down_proj(p, nq):
        rows, cols = slice(p * down_rows, (p + 1) * down_rows), slice(nq * piece_w, (nq + 1) * piece_w)
        o_ref[rows, cols] += jnp.dot(vt_ref[rows, :], act_scr[prev_slot, :, cols],
                                     preferred_element_type=F32)

    def routing(ig, lc, jb):
        lanes = slice(lc * LANES, (lc + 1) * LANES)
        js = slice(jb * PEER_JB, (jb + 1) * PEER_JB)
        wgt = [None] * PEER_IG
        for h in range(PEER_HEADS):
            s2 = s2_ref[h, js, lanes]
            bb = b_ref[h, js, lanes]
            tau = tau_ref[h, :, lanes]
            for ii in range(PEER_IG):
                i_loc = ig * PEER_IG + ii
                sel = jnp.where(s1_ref[h, i_loc:i_loc + 1, lanes] + s2 >= tau, bb, 0.0)
                term = a_ref[h, i_loc:i_loc + 1, lanes] * sel
                wgt[ii] = term if wgt[ii] is None else wgt[ii] + term
        for ii in range(PEER_IG):
            r0 = (ig * PEER_IG + ii) * N_KEYS + jb * PEER_JB
            w_scr[r0:r0 + PEER_JB, lanes] = wgt[ii]

    def activate(p, nq):
        rows, cols = slice(p * up_rows, (p + 1) * up_rows), slice(nq * piece_w, (nq + 1) * piece_w)
        act_scr[cur_slot, rows, cols] = (_gelu_tanh(hid_scr[rows, cols]) * w_scr[rows, cols]).astype(BF16)

    n_q = tn // piece_w
    mxu_work = ([functools.partial(up_proj, p, nq) for p in range(PEER_UP_PIECES) for nq in range(n_q)]
                + [functools.partial(down_proj, p, nq) for p in range(PEER_DOWN_PIECES) for nq in range(n_q)])
    vpu_work = []
    groups_per_piece = keys_per_step // PEER_UP_PIECES // PEER_IG
    for p in range(PEER_UP_PIECES):
        for nq in range(n_q):
            for lc in range(nq * piece_w // LANES, (nq + 1) * piece_w // LANES):
                vpu_work += [functools.partial(routing, ig, lc, jb)
                             for ig in range(p * groups_per_piece, (p + 1) * groups_per_piece)
                             for jb in range(N_KEYS // PEER_JB)]
            vpu_work.append(functools.partial(activate, p, nq))
    emitted = 0
    for j, work in enumerate(vpu_work):
        while emitted * len(vpu_work) <= j * len(mxu_work) and emitted < len(mxu_work):
            mxu_work[emitted]()
            emitted += 1
        work()
    for work in mxu_work[emitted:]:
        work()


def _peer_dense(xn, u_bf, vt_bf, s1, a, s2, b, tau):
    n = xn.shape[1]
    tn = min(n, 512)
    spt = N_EXPERTS // PEER_ET
    total = (n // tn) * spt

    def cur(s):
        return jnp.minimum(s, total - 1)

    def prev(s):
        return jnp.maximum(s - 1, 0)

    hk_spec = pl.BlockSpec((PEER_HEADS, N_KEYS, tn), lambda s: (0, 0, cur(s) // spt))
    hi_spec = pl.BlockSpec((PEER_HEADS, PEER_KEY_BLOCK, tn), lambda s: (0, cur(s) % spt, cur(s) // spt))
    return pl.pallas_call(
        functools.partial(_peer_dense_kernel, tn=tn, steps_per_tile=spt),
        grid=(total + 1,),
        in_specs=[pl.BlockSpec((D_MODEL, tn), lambda s: (0, cur(s) // spt)),
                  pl.BlockSpec((PEER_ET, D_MODEL), lambda s: (cur(s) % spt, 0)),
                  pl.BlockSpec((D_MODEL, PEER_ET), lambda s: (0, prev(s) % spt)),
                  hi_spec, hi_spec, hk_spec, hk_spec,
                  pl.BlockSpec((PEER_HEADS, 1, tn), lambda s: (0, 0, cur(s) // spt))],
        out_specs=pl.BlockSpec((D_MODEL, tn), lambda s: (0, prev(s) // spt)),
        out_shape=jax.ShapeDtypeStruct((D_MODEL, n), F32),
        scratch_shapes=[pltpu.VMEM((2, PEER_ET, tn), BF16),
                        pltpu.VMEM((PEER_ET, tn), F32),
                        pltpu.VMEM((PEER_ET, tn), F32)],
        compiler_params=_params("arbitrary"),
        name="peer_dense",
    )(xn, u_bf, vt_bf, s1, a, s2, b, tau)


def _final_kernel(x1_ref, pt_ref, g_ref, y_ref):
    x = x1_ref[...] + pt_ref[...].T
    y_ref[...] = x * lax.rsqrt(jnp.mean(x * x, axis=-1, keepdims=True) + EPS) * g_ref[...]


def _final(x1, peer_t, g):
    n = x1.shape[0]
    tn = min(n, 256)
    return pl.pallas_call(
        _final_kernel,
        grid=(n // tn,),
        in_specs=[pl.BlockSpec((tn, D_MODEL), lambda i: (i, 0)),
                  pl.BlockSpec((D_MODEL, tn), lambda i: (0, i)),
                  pl.BlockSpec((1, D_MODEL), lambda i: (0, 0))],
        out_specs=pl.BlockSpec((tn, D_MODEL), lambda i: (i, 0)),
        out_shape=jax.ShapeDtypeStruct((n, D_MODEL), F32),
        compiler_params=_params("arbitrary"),
        name="final",
    )(x1, peer_t, g)


def _prep_weights(norm_mix_g, w_in, conv_dw_w, conv_dw_b, conv_ln_g, conv_ln_b, conv_pw2, dn_conv_w,
                  dn_a_log, dn_dt_bias, dn_norm_g, dn_w_o, w_out, norm_ffn_g, peer_w_q, peer_sub_keys,
                  peer_u, peer_v, norm_final_g):
    w = w_in[0].astype(BF16)
    glu_end = 2 * CONV_DIM
    qkv_end = glu_end + DN_CONV_CH
    ba_end = qkv_end + 2 * DN_HEADS
    z_end = ba_end + DN_VAL
    w_pack = jnp.concatenate(
        [w[:, glu_end:qkv_end], w[:, :glu_end], w[:, ba_end:z_end], w[:, z_end:], w[:, qkv_end:ba_end],
         jnp.zeros((D_MODEL, LANES - 2 * DN_HEADS), w.dtype)], axis=1).astype(BF16)
    pad_lo = jnp.zeros((DN_HEADS,), F32)
    pad_hi = jnp.zeros((LANES - 2 * DN_HEADS,), F32)
    return dict(
        norm_mix_g=norm_mix_g[0][None, :], w_pack=w_pack,
        conv_dw_w=conv_dw_w[0], conv_dw_b=conv_dw_b[0][None, :],
        conv_ln_g=conv_ln_g[0][None, :], conv_ln_b=conv_ln_b[0][None, :],
        conv_pw2=conv_pw2[0].astype(BF16), dn_conv_w=dn_conv_w[0],
        alog_pad=jnp.concatenate([pad_lo, dn_a_log[0], pad_hi])[None, :],
        dt_pad=jnp.concatenate([pad_lo, dn_dt_bias[0], pad_hi])[None, :],
        dn_norm_g=dn_norm_g[0][None, :], dn_w_o=dn_w_o[0].astype(BF16), w_out=w_out[0].astype(BF16),
        norm_ffn_g=norm_ffn_g[0][None, :], peer_w_q=peer_w_q[0].astype(BF16),
        peer_keys=peer_sub_keys[0].reshape(2 * PEER_HEADS, N_KEYS, PEER_HALF).astype(BF16),
        peer_u=peer_u[0].astype(BF16), peer_vt=peer_v[0].astype(BF16).T,
        norm_final_g=norm_final_g[None, :])


def _trunk(x, conv_st, dconv_st, dn_st, wd, chunk):
    bsz, t_len, _ = x.shape
    n = bsz * t_len
    x2 = x.reshape(n, D_MODEL)
    proj = _in_proj(x2, wd["norm_mix_g"], wd["w_pack"])
    c_act, conv_new = _conv_branch(proj, conv_st, wd["conv_dw_w"], wd["conv_dw_b"], wd["conv_ln_g"],
                                   wd["conv_ln_b"], bsz, t_len)
    qkv, bg, dconv_new = _dn_prep(proj, dconv_st, wd["dn_conv_w"], wd["alog_pad"], wd["dt_pad"], bsz, t_len)
    beta = bg[:, :DN_HEADS]
    g = bg[:, DN_HEADS:2 * DN_HEADS]
    o, s_new = _gated_delta(qkv, beta, g, dn_st, bsz, t_len, chunk)
    x1 = _merge(c_act, o, proj, x2, wd["dn_norm_g"], wd["conv_pw2"], wd["dn_w_o"], wd["w_out"])
    xn2, s1, a, s2, b, tau = _peer_score(x1, wd["norm_ffn_g"], wd["peer_w_q"], wd["peer_keys"])
    peer_t = _peer_dense(xn2, wd["peer_u"], wd["peer_vt"], s1, a, s2, b, tau)
    y = _final(x1, peer_t, wd["norm_final_g"])
    return y.reshape(bsz, t_len, D_MODEL), conv_new[None], dconv_new[None], s_new[None]


def kernel(x_prompt, x_sample, state_conformer_conv, state_delta_conv, state_delta, norm_mix_g, w_in, conv_dw_w, conv_dw_b, conv_ln_g, conv_ln_b, conv_pw2, dn_conv_w, dn_a_log, dn_dt_bias, dn_norm_g, dn_w_o, w_out, norm_ffn_g, peer_w_q, peer_sub_keys, peer_u, peer_v, norm_final_g):
    wd = _prep_weights(norm_mix_g, w_in, conv_dw_w, conv_dw_b, conv_ln_g, conv_ln_b, conv_pw2, dn_conv_w,
                       dn_a_log, dn_dt_bias, dn_norm_g, dn_w_o, w_out, norm_ffn_g, peer_w_q, peer_sub_keys,
                       peer_u, peer_v, norm_final_g)
    bp = x_prompt.shape[0]
    zc = jnp.zeros((bp, CONV_WIDTH - 1, CONV_DIM), F32)
    zdc = jnp.zeros((bp, DN_CONV_WIDTH - 1, DN_CONV_CH), F32)
    zs = jnp.zeros((bp, DN_HEADS, DN_DK, DN_DV), F32)
    y_p, pc, pdc, ps = _trunk(x_prompt, zc, zdc, zs, wd, 64)
    y_s, sc, sdc, ss = _trunk(x_sample, state_conformer_conv[0], state_delta_conv[0], state_delta[0], wd,
                              x_sample.shape[1])
    return (y_p, y_s, pc, pdc, ps, sc, sdc, ss)
```

```python
import functools
import math

import jax
import jax.numpy as jnp
from jax import lax
from jax.experimental import pallas as pl
from jax.experimental.pallas import tpu as pltpu

F32 = jnp.float32
BF16 = jnp.bfloat16
HI = lax.Precision.HIGHEST

EPS = 1e-6
D_MODEL = 2048
CONV_DIM = 1024
CONV_WIDTH = 31
DN_HEADS = 8
DN_DK = 128
DN_DV = 128
DN_KEY = DN_HEADS * DN_DK
DN_VAL = DN_HEADS * DN_DV
DN_CONV_WIDTH = 4
DN_CONV_CH = 2 * DN_KEY + DN_VAL
PEER_HEADS = 8
PEER_HALF = 128
N_KEYS = 128
N_EXPERTS = N_KEYS * N_KEYS
PEER_TOPK = 16

LANES = 128
SUBLANES = 8
VMEM_LIMIT = 56 * 1024 * 1024

COL_QKV = 0
COL_GLU = COL_QKV + DN_CONV_CH
COL_Z = COL_GLU + 2 * CONV_DIM
COL_GATE = COL_Z + DN_VAL
COL_BA = COL_GATE + 2 * D_MODEL
PROJ_COLS = COL_BA + LANES


def _params(*sem, flags=None):
    return pltpu.CompilerParams(dimension_semantics=sem, vmem_limit_bytes=VMEM_LIMIT, flags=flags)


def _const_spec(shape):
    nd = len(shape)
    return pl.BlockSpec(shape, lambda *_: (0,) * nd, pipeline_mode=pl.Buffered(1))


def _sigmoid(x):
    return 0.5 * (1.0 + jnp.tanh(0.5 * x))


def _silu(x):
    return x * _sigmoid(x)


def _in_proj_kernel(x_ref, g_ref, w_ref, o_ref, xn_ref, *, rows):
    @pl.when(pl.program_id(1) == 0)
    def _():
        def body(i, carry):
            r = pl.ds(pl.multiple_of(i * rows, rows), rows)
            x = x_ref[r, :]
            y = x * lax.rsqrt(jnp.mean(x * x, axis=-1, keepdims=True) + EPS)
            xn_ref[r, :] = (y * g_ref[...]).astype(BF16)
            return carry
        lax.fori_loop(0, x_ref.shape[0] // rows, body, 0)

    o_ref[...] = jnp.dot(xn_ref[...], w_ref[...], preferred_element_type=F32)


def _in_proj(x2, g, w_pack):
    n = x2.shape[0]
    tm = min(n, 1024)
    tn = 1152
    return pl.pallas_call(
        functools.partial(_in_proj_kernel, rows=min(tm, 64)),
        grid=(n // tm, PROJ_COLS // tn),
        in_specs=[pl.BlockSpec((tm, D_MODEL), lambda i, j: (i, 0)),
                  pl.BlockSpec((1, D_MODEL), lambda i, j: (0, 0)),
                  pl.BlockSpec((D_MODEL, tn), lambda i, j: (0, j))],
        out_specs=pl.BlockSpec((tm, tn), lambda i, j: (i, j)),
        out_shape=jax.ShapeDtypeStruct((n, PROJ_COLS), F32),
        scratch_shapes=[pltpu.VMEM((tm, D_MODEL), BF16)],
        compiler_params=_params("arbitrary", "arbitrary"),
        name="in_proj",
    )(x2, g, w_pack)


def _conv_rows(xp_ref, w_ref, r0, rows, lanes, first, kw):
    acc = None
    k = 0
    while k < kw:
        off = first + k
        base = (off // SUBLANES) * SUBLANES
        nk = min(kw - k, base + SUBLANES - off)
        extra = SUBLANES if off + nk - 1 > base else 0
        win = xp_ref[pl.ds(pl.multiple_of(r0 + base, SUBLANES), rows + extra), lanes]
        for kk in range(k, k + nk):
            s = first + kk - base
            term = w_ref[kk:kk + 1, lanes] * win[s:s + rows, :]
            acc = term if acc is None else acc + term
        k += nk
    return acc


def _conv_rows_by_residue(xp_ref, w_ref, r0, rows, lanes, first, kw):
    y = None
    for b in range(SUBLANES):
        taps = [k for k in range(kw) if (first + k) % SUBLANES == b]
        if not taps:
            continue
        span = rows + (SUBLANES if b else 0)
        z = None
        for k in taps:
            start = pl.multiple_of(r0 + (first + k) // SUBLANES * SUBLANES, SUBLANES)
            term = w_ref[k:k + 1, lanes] * xp_ref[pl.ds(start, span), lanes]
            z = term if z is None else z + term
        z = z[b:b + rows, :]
        y = z if y is None else y + z
    return y


CONV_HDR = 32
CONV_LANE_CHUNK = 256


def _conv_branch_kernel(a_ref, b_ref, prev_ref, w_ref, bias_ref, lng_ref, lnb_ref,
                        c_ref, new_ref, xp_ref, y_ref, *, tt, rows):
    t = pl.program_id(1)
    p = CONV_WIDTH - 1

    xp_ref[0:CONV_HDR - p, :] = jnp.zeros((CONV_HDR - p, CONV_DIM), F32)

    @pl.when(t == 0)
    def _():
        xp_ref[CONV_HDR - p:CONV_HDR, :] = prev_ref[0]

    @pl.when(t > 0)
    def _():
        xp_ref[CONV_HDR - p:CONV_HDR, :] = xp_ref[CONV_HDR + tt - p:CONV_HDR + tt, :]

    xp_ref[CONV_HDR:CONV_HDR + tt, :] = a_ref[...] * _sigmoid(b_ref[...])

    def body(i, carry):
        r0 = pl.multiple_of(i * rows, rows)
        for lc in range(CONV_DIM // CONV_LANE_CHUNK):
            lanes = slice(lc * CONV_LANE_CHUNK, (lc + 1) * CONV_LANE_CHUNK)
            y_ref[:, lanes] = _conv_rows_by_residue(xp_ref, w_ref, r0, rows, lanes, CONV_HDR - p, CONV_WIDTH)
        y = y_ref[...] + bias_ref[...]
        mu = jnp.mean(y, axis=-1, keepdims=True)
        yc = y - mu
        var = jnp.mean(yc * yc, axis=-1, keepdims=True)
        z = yc * lax.rsqrt(var + EPS) * lng_ref[...] + lnb_ref[...]
        c_ref[pl.ds(r0, rows), :] = _silu(z).astype(BF16)
        return carry
    lax.fori_loop(0, tt // rows, body, 0)

    new_ref[0] = xp_ref[CONV_HDR + tt - p:CONV_HDR + tt, :]


def _conv_branch(proj, conv_prev, w, bias, lng, lnb, bsz, t_len):
    tt = min(t_len, 256)
    nt = t_len // tt
    p = CONV_WIDTH - 1
    blk_a = COL_GLU // CONV_DIM
    rows = min(tt, 32)
    return pl.pallas_call(
        functools.partial(_conv_branch_kernel, tt=tt, rows=rows),
        grid=(bsz, nt),
        in_specs=[pl.BlockSpec((tt, CONV_DIM), lambda b, t: (b * nt + t, blk_a)),
                  pl.BlockSpec((tt, CONV_DIM), lambda b, t: (b * nt + t, blk_a + 1)),
                  pl.BlockSpec((1, p, CONV_DIM), lambda b, t: (b, 0, 0)),
                  pl.BlockSpec((CONV_WIDTH, CONV_DIM), lambda b, t: (0, 0)),
                  pl.BlockSpec((1, CONV_DIM), lambda b, t: (0, 0)),
                  pl.BlockSpec((1, CONV_DIM), lambda b, t: (0, 0)),
                  pl.BlockSpec((1, CONV_DIM), lambda b, t: (0, 0))],
        out_specs=[pl.BlockSpec((tt, CONV_DIM), lambda b, t: (b * nt + t, 0)),
                   pl.BlockSpec((1, p, CONV_DIM), lambda b, t: (b, 0, 0))],
        out_shape=[jax.ShapeDtypeStruct((bsz * t_len, CONV_DIM), BF16),
                   jax.ShapeDtypeStruct((bsz, p, CONV_DIM), F32)],
        scratch_shapes=[pltpu.VMEM((CONV_HDR + tt, CONV_DIM), F32),
                        pltpu.VMEM((rows, CONV_DIM), F32)],
        compiler_params=_params("arbitrary", "arbitrary"),
        name="conv_branch",
    )(proj, proj, conv_prev, w, bias, lng, lnb)


DN_HDR = 8


def _dn_prep_kernel(x_ref, ba_ref, prev_ref, w_ref, alog_ref, dt_ref,
                    qkv_ref, bg_ref, new_ref, xp_ref, *, tt, rows):
    t = pl.program_id(1)
    p = DN_CONV_WIDTH - 1

    xp_ref[0:DN_HDR - p, :] = jnp.zeros((DN_HDR - p, DN_CONV_CH), F32)

    @pl.when(t == 0)
    def _():
        xp_ref[DN_HDR - p:DN_HDR, :] = prev_ref[0]

    @pl.when(t > 0)
    def _():
        xp_ref[DN_HDR - p:DN_HDR, :] = xp_ref[DN_HDR + tt - p:DN_HDR + tt, :]

    xp_ref[DN_HDR:DN_HDR + tt, :] = x_ref[...]

    def body(i, carry):
        r0 = pl.multiple_of(i * rows, rows)
        for hs in range(DN_CONV_CH // LANES):
            lanes = slice(hs * LANES, (hs + 1) * LANES)
            y = _silu(_conv_rows(xp_ref, w_ref, r0, rows, lanes, DN_HDR - p, DN_CONV_WIDTH))
            if hs < 2 * DN_HEADS:
                y = y * lax.rsqrt(jnp.sum(y * y, axis=-1, keepdims=True) + EPS)
                if hs < DN_HEADS:
                    y = y * (DN_DK ** -0.5)
            qkv_ref[pl.ds(r0, rows), lanes] = y
        return carry
    lax.fori_loop(0, tt // rows, body, 0)

    raw = ba_ref[...]
    sp_in = raw + dt_ref[...]
    softplus = jnp.maximum(sp_in, 0.0) + jnp.log(1.0 + jnp.exp(-jnp.abs(sp_in)))
    lane = lax.broadcasted_iota(jnp.int32, raw.shape, 1)
    bg_ref[...] = jnp.where(lane < DN_HEADS, _sigmoid(raw), -jnp.exp(alog_ref[...]) * softplus)

    new_ref[0] = xp_ref[DN_HDR + tt - p:DN_HDR + tt, :]


def _dn_prep(proj, dconv_prev, w, alog_pad, dt_pad, bsz, t_len):
    tt = min(t_len, 256)
    nt = t_len // tt
    p = DN_CONV_WIDTH - 1
    return pl.pallas_call(
        functools.partial(_dn_prep_kernel, tt=tt, rows=16),
        grid=(bsz, nt),
        in_specs=[pl.BlockSpec((tt, DN_CONV_CH), lambda b, t: (b * nt + t, COL_QKV // DN_CONV_CH)),
                  pl.BlockSpec((tt, LANES), lambda b, t: (b * nt + t, COL_BA // LANES)),
                  pl.BlockSpec((1, p, DN_CONV_CH), lambda b, t: (b, 0, 0)),
                  pl.BlockSpec((DN_CONV_WIDTH, DN_CONV_CH), lambda b, t: (0, 0)),
                  pl.BlockSpec((1, LANES), lambda b, t: (0, 0)),
                  pl.BlockSpec((1, LANES), lambda b, t: (0, 0))],
        out_specs=[pl.BlockSpec((tt, DN_CONV_CH), lambda b, t: (b * nt + t, 0)),
                   pl.BlockSpec((tt, LANES), lambda b, t: (b * nt + t, 0)),
                   pl.BlockSpec((1, p, DN_CONV_CH), lambda b, t: (b, 0, 0))],
        out_shape=[jax.ShapeDtypeStruct((bsz * t_len, DN_CONV_CH), F32),
                   jax.ShapeDtypeStruct((bsz * t_len, LANES), F32),
                   jax.ShapeDtypeStruct((bsz, p, DN_CONV_CH), F32)],
        scratch_shapes=[pltpu.VMEM((DN_HDR + tt, DN_CONV_CH), F32)],
        compiler_params=_params("arbitrary", "arbitrary"),
        name="dn_prep",
    )(proj, proj, dconv_prev, w, alog_pad, dt_pad)


def _dot_hi(a, b):
    return jnp.dot(a, b, precision=HI, preferred_element_type=F32)


_NN = (((1,), (0,)), ((), ()))
_NT = (((1,), (1,)), ((), ()))


def _split(a):
    hi = a.astype(BF16)
    lo = (a - hi.astype(F32)).astype(BF16)
    return hi, lo


def _mm3(a, b, dims=_NN, passes=3):
    a_hi, a_lo = a
    b_hi, b_lo = b
    out = lax.dot_general(a_hi, b_hi, dims, preferred_element_type=F32)
    if passes == 3:
        out = (out + lax.dot_general(a_hi, b_lo, dims, preferred_element_type=F32)
               + lax.dot_general(a_lo, b_hi, dims, preferred_element_type=F32))
    return out


PASSES_GRAM = 1
PASSES_INVERSE = 1
PASSES_STATE = 1


def _gated_delta_kernel(q_ref, k_ref, v_ref, bcol_ref, gcol_ref, grow_ref, s0_ref, o_ref, s_ref, *, chunk):
    c = pl.program_id(1)

    @pl.when(c == 0)
    def _():
        s_ref[...] = s0_ref[...]

    row = lax.broadcasted_iota(jnp.int32, (chunk, chunk), 0)
    col = lax.broadcasted_iota(jnp.int32, (chunk, chunk), 1)
    causal = row >= col
    strict = row > col
    lower = causal.astype(F32)
    upper = (row <= col).astype(F32)
    eye = (row == col).astype(F32)

    n_b = q_ref.shape[0]
    g_cum_col = [_dot_hi(lower, gcol_ref[b]) for b in range(n_b)]
    g_cum_row = [_dot_hi(grow_ref[b, 0], upper) for b in range(n_b)]
    n_sq = max(1, int(math.ceil(math.log2(chunk))) - 1)

    chains = [(b, hh) for b in range(n_b) for hh in range(DN_HEADS)]
    heads = range(len(chains))
    lanes = [slice(hh * DN_DK, (hh + 1) * DN_DK) for _, hh in chains]
    q = [q_ref[b, :, lanes[h]] for h, (b, _) in enumerate(chains)]
    k = [k_ref[b, :, lanes[h]] for h, (b, _) in enumerate(chains)]
    v = [v_ref[b, :, lanes[h]] for h, (b, _) in enumerate(chains)]
    beta = [bcol_ref[b, :, hh:hh + 1] for b, hh in chains]
    gc = [g_cum_col[b][:, hh:hh + 1] for b, hh in chains]
    gr = [g_cum_row[b][hh:hh + 1, :] for b, hh in chains]
    g_last = [gr[h][:, chunk - 1:chunk] for h in heads]
    k_sp = [_split(k[h]) for h in heads]
    q_sp = [_split(q[h]) for h in heads]
    s_old = [s_ref[b, hh] for b, hh in chains]
    s_sp = [_split(s_old[h]) for h in heads]
    decay = [jnp.where(causal, jnp.exp(jnp.where(causal, gc[h] - gr[h], 0.0)), 0.0) for h in heads]
    a_mat = [jnp.where(strict, beta[h] * _mm3(k_sp[h], k_sp[h], _NT, PASSES_GRAM) * decay[h], 0.0)
             for h in heads]
    qk = [_mm3(q_sp[h], k_sp[h], _NT, PASSES_GRAM) * decay[h] for h in heads]
    gam = [jnp.exp(gc[h]) for h in heads]
    qg_s = [_mm3(_split(q[h] * gam[h]), s_sp[h], _NN, PASSES_STATE) for h in heads]
    inv = [eye - a_mat[h] for h in heads]
    a_pow = [_split(a_mat[h]) for h in heads]
    for _ in range(n_sq):
        a_pow = [_split(_mm3(a_pow[h], a_pow[h], _NN, PASSES_INVERSE)) for h in heads]
        inv = [inv[h] + _mm3(_split(inv[h]), a_pow[h], _NN, PASSES_INVERSE) for h in heads]
    rhs = [jnp.concatenate([beta[h] * v[h], (beta[h] * gam[h]) * k[h]], axis=1) for h in heads]
    sol = [_mm3(_split(inv[h]), _split(rhs[h]), _NN, PASSES_INVERSE) for h in heads]
    delta = [sol[h][:, :DN_DV] - _mm3(_split(sol[h][:, DN_DV:]), s_sp[h], _NN, PASSES_STATE) for h in heads]
    d_sp = [_split(delta[h]) for h in heads]
    for h, (b, _) in enumerate(chains):
        o_ref[b, :, lanes[h]] = qg_s[h] + _mm3(_split(qk[h]), d_sp[h], _NN, PASSES_STATE)
    for h, (b, hh) in enumerate(chains):
        kd_t = (k[h] * jnp.exp(g_last[h] - gc[h])).T
        s_ref[b, hh] = jnp.exp(g_last[h]) * s_old[h] + _mm3(_split(kd_t), d_sp[h], _NN, PASSES_STATE)


DELTA_BATCH = 2


def _gated_delta(qkv, beta, g, s0, bsz, t_len, chunk):
    nc = t_len // chunk
    nb = DELTA_BATCH
    qkv3 = qkv.reshape(bsz, t_len, DN_CONV_CH)
    beta3 = beta.reshape(bsz, t_len, DN_HEADS)
    g3 = g.reshape(bsz, t_len, DN_HEADS)
    g_row = jnp.swapaxes(g.reshape(bsz, nc, chunk, DN_HEADS), 2, 3)
    o, s_new = pl.pallas_call(
        functools.partial(_gated_delta_kernel, chunk=chunk),
        grid=(bsz // nb, nc),
        in_specs=[pl.BlockSpec((nb, chunk, DN_KEY), lambda b, c: (b, c, 0)),
                  pl.BlockSpec((nb, chunk, DN_KEY), lambda b, c: (b, c, 1)),
                  pl.BlockSpec((nb, chunk, DN_VAL), lambda b, c: (b, c, 2)),
                  pl.BlockSpec((nb, chunk, DN_HEADS), lambda b, c: (b, c, 0)),
                  pl.BlockSpec((nb, chunk, DN_HEADS), lambda b, c: (b, c, 0)),
                  pl.BlockSpec((nb, 1, DN_HEADS, chunk), lambda b, c: (b, c, 0, 0)),
                  pl.BlockSpec((nb, DN_HEADS, DN_DK, DN_DV), lambda b, c: (b, 0, 0, 0))],
        out_specs=[pl.BlockSpec((nb, chunk, DN_VAL), lambda b, c: (b, c, 0)),
                   pl.BlockSpec((nb, DN_HEADS, DN_DK, DN_DV), lambda b, c: (b, 0, 0, 0))],
        out_shape=[jax.ShapeDtypeStruct((bsz, t_len, DN_VAL), F32),
                   jax.ShapeDtypeStruct((bsz, DN_HEADS, DN_DK, DN_DV), F32)],
        compiler_params=_params("arbitrary", "arbitrary"),
        name="gated_delta",
    )(qkv3, qkv3, qkv3, beta3, g3, g_row, s0)
    return o.reshape(bsz * t_len, DN_VAL), s_new


def _merge_kernel(c_ref, o_ref, z_ref, gc_ref, gd_ref, x_ref, ng_ref, pw2_ref, wo_ref, wout_ref,
                  x1_ref, og_ref):
    for h in range(DN_HEADS):
        lanes = slice(h * DN_DV, (h + 1) * DN_DV)
        oh = o_ref[:, lanes]
        oh = oh * lax.rsqrt(jnp.mean(oh * oh, axis=-1, keepdims=True) + EPS) * ng_ref[...]
        og_ref[:, lanes] = (oh * _silu(z_ref[:, lanes])).astype(BF16)
    y_dn = jnp.dot(og_ref[...], wo_ref[...], preferred_element_type=F32)
    y_conv = jnp.dot(c_ref[...], pw2_ref[...], preferred_element_type=F32)
    mix = _sigmoid(gc_ref[...]) * y_conv + _sigmoid(gd_ref[...]) * y_dn
    x1_ref[...] = x_ref[...] + jnp.dot(mix.astype(BF16), wout_ref[...], preferred_element_type=F32)


def _merge(c, o, proj, x2, ng, pw2, wo, wout):
    n = x2.shape[0]
    tm = min(n, 256)
    return pl.pallas_call(
        _merge_kernel,
        grid=(n // tm,),
        in_specs=[pl.BlockSpec((tm, CONV_DIM), lambda i: (i, 0)),
                  pl.BlockSpec((tm, DN_VAL), lambda i: (i, 0)),
                  pl.BlockSpec((tm, DN_VAL), lambda i: (i, COL_Z // DN_VAL)),
                  pl.BlockSpec((tm, D_MODEL), lambda i: (i, COL_GATE // D_MODEL)),
                  pl.BlockSpec((tm, D_MODEL), lambda i: (i, COL_GATE // D_MODEL + 1)),
                  pl.BlockSpec((tm, D_MODEL), lambda i: (i, 0)),
                  _const_spec((1, DN_DV)),
                  _const_spec((CONV_DIM, D_MODEL)),
                  _const_spec((DN_VAL, D_MODEL)),
                  _const_spec((D_MODEL, D_MODEL))],
        out_specs=pl.BlockSpec((tm, D_MODEL), lambda i: (i, 0)),
        out_shape=jax.ShapeDtypeStruct((n, D_MODEL), F32),
        scratch_shapes=[pltpu.VMEM((tm, DN_VAL), BF16)],
        compiler_params=_params("arbitrary"),
        name="merge",
    )(c, o, proj, proj, proj, x2, ng, pw2, wo, wout)


CAND_ROWS = 8 * 8 + 8 + 8


PEER_ET = 1024
PEER_KEY_BLOCK = SUBLANES
PEER_KEYS_PER_STEP = PEER_ET // N_KEYS
PEER_KEY_ROWS = N_KEYS // PEER_KEYS_PER_STEP * PEER_KEY_BLOCK


def _store_key_groups(dst_ref, h, lanes, x):
    for g in range(N_KEYS // PEER_KEYS_PER_STEP):
        dst_ref[h, g * PEER_KEY_BLOCK:g * PEER_KEY_BLOCK + PEER_KEYS_PER_STEP, lanes] = (
            x[g * PEER_KEYS_PER_STEP:(g + 1) * PEER_KEYS_PER_STEP, :])


def _sorting_network(n):
    size = 1
    while size < n:
        size *= 2
    pairs = []
    p = 1
    while p < size:
        k = p
        while k >= 1:
            for j in range(k % p, size - k, 2 * k):
                for i in range(min(k, size - j - k)):
                    if (i + j) // (2 * p) == (i + j + k) // (2 * p) and i + j + k < n:
                        pairs.append((i + j, i + j + k))
            k //= 2
        p *= 2
    return pairs


def _top_rows(xs, dsts, k):
    slabs = [[x[i:i + SUBLANES, :] for i in range(0, x.shape[0], SUBLANES)] for x in xs]
    for s in slabs:
        for i, j in _sorting_network(len(s)):
            s[i], s[j] = jnp.maximum(s[i], s[j]), jnp.minimum(s[i], s[j])
    for r in range(k):
        for c, (dst_ref, lanes) in enumerate(dsts):
            s = slabs[c]
            m = s[0]
            for shift in (4, 2, 1):
                m = jnp.maximum(m, pltpu.roll(m, shift, 0))
            dst_ref[r:r + 1, lanes] = m[0:1, :]
            popped = s[0] == m
            for d in range(min(len(s), k - 1 - r)):
                below = s[d + 1] if d + 1 < len(s) else -jnp.inf
                s[d] = jnp.where(popped, below, s[d])


PEER_HEADS_PER_ITER = 8


def _peer_score_kernel(x1_ref, g_ref, wq_ref, keys_ref, xn_ref, s1_ref, a_ref, s2_ref, b_ref, tau_ref,
                       q_scr, c1_all, c2_all, cand_all, top_all, *, tn):
    x = x1_ref[...]
    xn = x * lax.rsqrt(jnp.mean(x * x, axis=-1, keepdims=True) + EPS) * g_ref[...]
    xn_ref[...] = xn.T.astype(BF16)
    q = jnp.dot(xn.astype(BF16), wq_ref[...], preferred_element_type=F32)
    for hp in range(2 * PEER_HEADS):
        q_scr[hp] = q[:, hp * PEER_HALF:(hp + 1) * PEER_HALF].astype(BF16)

    def head(h, slot):
        c1_scr, c2_scr, cand_scr, top_scr = (c1_all.at[slot], c2_all.at[slot], cand_all.at[slot],
                                             top_all.at[slot])
        s1 = lax.dot_general(keys_ref[2 * h], q_scr[2 * h], (((1,), (1,)), ((), ())),
                             preferred_element_type=F32)
        s2 = lax.dot_general(keys_ref[2 * h + 1], q_scr[2 * h + 1], (((1,), (1,)), ((), ())),
                             preferred_element_type=F32)
        if PEER_KEYS_PER_STEP < PEER_KEY_BLOCK:
            s1_ref[h] = jnp.zeros(s1_ref.shape[1:], F32)
            a_ref[h] = jnp.zeros(a_ref.shape[1:], F32)
        _store_key_groups(s1_ref, h, slice(None), s1)
        s2_ref[h] = s2
        chunks = [slice(lc * LANES, (lc + 1) * LANES) for lc in range(tn // LANES)]
        for lanes in chunks:
            _top_rows([s1[:, lanes], s2[:, lanes]], [(c1_scr, lanes), (c2_scr, lanes)], PEER_TOPK)
        for lanes in chunks:
            c1 = c1_scr[:, lanes]
            c2 = c2_scr[:, lanes]
            for r1 in range(8):
                cand_scr[r1 * 8:(r1 + 1) * 8, lanes] = c1[r1:r1 + 1, :] + c2[0:8, :]
            cand_scr[64:72, lanes] = c1[8:16, :] + c2[0:1, :]
            cand_scr[72:80, lanes] = c1[0:1, :] + c2[8:16, :]
        _top_rows([cand_scr[:, lanes] for lanes in chunks], [(top_scr, lanes) for lanes in chunks], PEER_TOPK)
        for lanes in chunks:
            top = top_scr[:, lanes]
            m0 = top[0:1, :]
            zsum = jnp.sum(jnp.exp(top - m0), axis=0, keepdims=True)
            tau_ref[h, :, lanes] = top[PEER_TOPK - 1:PEER_TOPK, :]
            _store_key_groups(a_ref, h, lanes, jnp.exp(s1[:, lanes] - c1_scr[0:1, lanes]))
            b_ref[h, :, lanes] = jnp.exp(s2[:, lanes] - c2_scr[0:1, lanes]) / zsum

    def head_group(i, carry):
        for slot in range(PEER_HEADS_PER_ITER):
            head(i * PEER_HEADS_PER_ITER + slot, slot)
        return carry
    lax.fori_loop(0, PEER_HEADS // PEER_HEADS_PER_ITER, head_group, 0)


def _peer_score(x1, g, wq, keys):
    n = x1.shape[0]
    tn = min(n, 256)
    hk = (PEER_HEADS, N_KEYS, n)
    hk_spec = pl.BlockSpec((PEER_HEADS, N_KEYS, tn), lambda i: (0, 0, i))
    hi = (PEER_HEADS, PEER_KEY_ROWS, n)
    hi_spec = pl.BlockSpec((PEER_HEADS, PEER_KEY_ROWS, tn), lambda i: (0, 0, i))
    return pl.pallas_call(
        functools.partial(_peer_score_kernel, tn=tn),
        grid=(n // tn,),
        in_specs=[pl.BlockSpec((tn, D_MODEL), lambda i: (i, 0)),
                  _const_spec((1, D_MODEL)),
                  _const_spec((D_MODEL, 2 * PEER_HEADS * PEER_HALF)),
                  _const_spec((2 * PEER_HEADS, N_KEYS, PEER_HALF))],
        out_specs=[pl.BlockSpec((D_MODEL, tn), lambda i: (0, i)),
                   hi_spec, hi_spec, hk_spec, hk_spec,
                   pl.BlockSpec((PEER_HEADS, 1, tn), lambda i: (0, 0, i))],
        out_shape=[jax.ShapeDtypeStruct((D_MODEL, n), BF16),
                   jax.ShapeDtypeStruct(hi, F32), jax.ShapeDtypeStruct(hi, F32),
                   jax.ShapeDtypeStruct(hk, F32), jax.ShapeDtypeStruct(hk, F32),
                   jax.ShapeDtypeStruct((PEER_HEADS, 1, n), F32)],
        scratch_shapes=[pltpu.VMEM((2 * PEER_HEADS, tn, PEER_HALF), BF16),
                        pltpu.VMEM((PEER_HEADS_PER_ITER, PEER_TOPK, tn), F32),
                        pltpu.VMEM((PEER_HEADS_PER_ITER, PEER_TOPK, tn), F32),
                        pltpu.VMEM((PEER_HEADS_PER_ITER, CAND_ROWS, tn), F32),
                        pltpu.VMEM((PEER_HEADS_PER_ITER, PEER_TOPK, tn), F32)],
        compiler_params=_params("arbitrary"),
        name="peer_score",
    )(x1, g, wq, keys)


PEER_UP_PIECES = 2
PEER_DOWN_PIECES = 4
PEER_IG = 2
PEER_JB = 64

_GELU_C = math.sqrt(2.0 / math.pi)


def _gelu_tanh(x):
    return 0.5 * x * (1.0 + jnp.tanh(_GELU_C * (x + 0.044715 * (x * x * x))))


def _peer_dense_kernel(xn_ref, u_ref, vt_ref, s1_ref, a_ref, s2_ref, b_ref, tau_ref, o_ref,
                       act_scr, w_scr, hid_scr, *, tn, steps_per_tile):
    s = pl.program_id(0)
    prev = jnp.maximum(s - 1, 0)
    cur_slot = s % 2
    prev_slot = 1 - cur_slot

    @pl.when(s == 0)
    def _():
        act_scr[1] = jnp.zeros(act_scr.shape[1:], BF16)

    @pl.when(prev % steps_per_tile == 0)
    def _():
        o_ref[...] = jnp.zeros_like(o_ref)

    keys_per_step = PEER_ET // N_KEYS
    piece_w = min(tn, 2 * LANES)
    up_rows = PEER_ET // PEER_UP_PIECES
    down_rows = D_MODEL // PEER_DOWN_PIECES

    def up_proj(p, nq):
        rows, cols = slice(p * up_rows, (p + 1) * up_rows), slice(nq * piece_w, (nq + 1) * piece_w)
        hid_scr[rows, cols] = jnp.dot(u_ref[rows, :], xn_ref[:, cols], preferred_element_type=F32)

    def down_proj(p, nq):
        rows, cols = slice(p * down_rows, (p + 1) * down_rows), slice(nq * piece_w, (nq + 1) * piece_w)
        o_ref[rows, cols] += jnp.dot(vt_ref[rows, :], act_scr[prev_slot, :, cols],
                                     preferred_element_type=F32)

    def routing(ig, lc, jb):
        lanes = slice(lc * LANES, (lc + 1) * LANES)
        js = slice(jb * PEER_JB, (jb + 1) * PEER_JB)
        wgt = [None] * PEER_IG
        for h in range(PEER_HEADS):
            s2 = s2_ref[h, js, lanes]
            bb = b_ref[h, js, lanes]
            tau = tau_ref[h, :, lanes]
            for ii in range(PEER_IG):
                i_loc = ig * PEER_IG + ii
                sel = jnp.where(s1_ref[h, i_loc:i_loc + 1, lanes] + s2 >= tau, bb, 0.0)
                term = a_ref[h, i_loc:i_loc + 1, lanes] * sel
                wgt[ii] = term if wgt[ii] is None else wgt[ii] + term
        for ii in range(PEER_IG):
            r0 = (ig * PEER_IG + ii) * N_KEYS + jb * PEER_JB
            w_scr[r0:r0 + PEER_JB, lanes] = wgt[ii]

    def activate(p, nq):
        rows, cols = slice(p * up_rows, (p + 1) * up_rows), slice(nq * piece_w, (nq + 1) * piece_w)
        act_scr[cur_slot, rows, cols] = (_gelu_tanh(hid_scr[rows, cols]) * w_scr[rows, cols]).astype(BF16)

    n_q = tn // piece_w
    mxu_work = ([functools.partial(up_proj, p, nq) for p in range(PEER_UP_PIECES) for nq in range(n_q)]
                + [functools.partial(down_proj, p, nq) for p in range(PEER_DOWN_PIECES) for nq in range(n_q)])
    vpu_work = []
    groups_per_piece = keys_per_step // PEER_UP_PIECES // PEER_IG
    for p in range(PEER_UP_PIECES):
        for nq in range(n_q):
            for lc in range(nq * piece_w // LANES, (nq + 1) * piece_w // LANES):
                vpu_work += [functools.partial(routing, ig, lc, jb)
                             for ig in range(p * groups_per_piece, (p + 1) * groups_per_piece)
                             for jb in range(N_KEYS // PEER_JB)]
            vpu_work.append(functools.partial(activate, p, nq))
    emitted = 0
    for j, work in enumerate(vpu_work):
        while emitted * len(vpu_work) <= j * len(mxu_work) and emitted < len(mxu_work):
            mxu_work[emitted]()
            emitted += 1
        work()
    for work in mxu_work[emitted:]:
        work()


def _peer_dense(xn, u_bf, vt_bf, s1, a, s2, b, tau):
    n = xn.shape[1]
    tn = min(n, 512)
    spt = N_EXPERTS // PEER_ET
    total = (n // tn) * spt

    def cur(s):
        return jnp.minimum(s, total - 1)

    def prev(s):
        return jnp.maximum(s - 1, 0)

    hk_spec = pl.BlockSpec((PEER_HEADS, N_KEYS, tn), lambda s: (0, 0, cur(s) // spt))
    hi_spec = pl.BlockSpec((PEER_HEADS, PEER_KEY_BLOCK, tn), lambda s: (0, cur(s) % spt, cur(s) // spt))
    return pl.pallas_call(
        functools.partial(_peer_dense_kernel, tn=tn, steps_per_tile=spt),
        grid=(total + 1,),
        in_specs=[pl.BlockSpec((D_MODEL, tn), lambda s: (0, cur(s) // spt)),
                  pl.BlockSpec((PEER_ET, D_MODEL), lambda s: (cur(s) % spt, 0)),
                  pl.BlockSpec((D_MODEL, PEER_ET), lambda s: (0, prev(s) % spt)),
                  hi_spec, hi_spec, hk_spec, hk_spec,
                  pl.BlockSpec((PEER_HEADS, 1, tn), lambda s: (0, 0, cur(s) // spt))],
        out_specs=pl.BlockSpec((D_MODEL, tn), lambda s: (0, prev(s) // spt)),
        out_shape=jax.ShapeDtypeStruct((D_MODEL, n), F32),
        scratch_shapes=[pltpu.VMEM((2, PEER_ET, tn), BF16),
                        pltpu.VMEM((PEER_ET, tn), F32),
                        pltpu.VMEM((PEER_ET, tn), F32)],
        compiler_params=_params("arbitrary"),
        name="peer_dense",
    )(xn, u_bf, vt_bf, s1, a, s2, b, tau)


def _final_kernel(x1_ref, pt_ref, g_ref, y_ref):
    x = x1_ref[...] + pt_ref[...].T
    y_ref[...] = x * lax.rsqrt(jnp.mean(x * x, axis=-1, keepdims=True) + EPS) * g_ref[...]


def _final(x1, peer_t, g):
    n = x1.shape[0]
    tn = min(n, 256)
    return pl.pallas_call(
        _final_kernel,
        grid=(n // tn,),
        in_specs=[pl.BlockSpec((tn, D_MODEL), lambda i: (i, 0)),
                  pl.BlockSpec((D_MODEL, tn), lambda i: (0, i)),
                  pl.BlockSpec((1, D_MODEL), lambda i: (0, 0))],
        out_specs=pl.BlockSpec((tn, D_MODEL), lambda i: (i, 0)),
        out_shape=jax.ShapeDtypeStruct((n, D_MODEL), F32),
        compiler_params=_params("arbitrary"),
        name="final",
    )(x1, peer_t, g)


def _prep_weights(norm_mix_g, w_in, conv_dw_w, conv_dw_b, conv_ln_g, conv_ln_b, conv_pw2, dn_conv_w,
                  dn_a_log, dn_dt_bias, dn_norm_g, dn_w_o, w_out, norm_ffn_g, peer_w_q, peer_sub_keys,
                  peer_u, peer_v, norm_final_g):
    w = w_in[0].astype(BF16)
    glu_end = 2 * CONV_DIM
    qkv_end = glu_end + DN_CONV_CH
    ba_end = qkv_end + 2 * DN_HEADS
    z_end = ba_end + DN_VAL
    w_pack = jnp.concatenate(
        [w[:, glu_end:qkv_end], w[:, :glu_end], w[:, ba_end:z_end], w[:, z_end:], w[:, qkv_end:ba_end],
         jnp.zeros((D_MODEL, LANES - 2 * DN_HEADS), w.dtype)], axis=1).astype(BF16)
    pad_lo = jnp.zeros((DN_HEADS,), F32)
    pad_hi = jnp.zeros((LANES - 2 * DN_HEADS,), F32)
    return dict(
        norm_mix_g=norm_mix_g[0][None, :], w_pack=w_pack,
        conv_dw_w=conv_dw_w[0], conv_dw_b=conv_dw_b[0][None, :],
        conv_ln_g=conv_ln_g[0][None, :], conv_ln_b=conv_ln_b[0][None, :],
        conv_pw2=conv_pw2[0].astype(BF16), dn_conv_w=dn_conv_w[0],
        alog_pad=jnp.concatenate([pad_lo, dn_a_log[0], pad_hi])[None, :],
        dt_pad=jnp.concatenate([pad_lo, dn_dt_bias[0], pad_hi])[None, :],
        dn_norm_g=dn_norm_g[0][None, :], dn_w_o=dn_w_o[0].astype(BF16), w_out=w_out[0].astype(BF16),
        norm_ffn_g=norm_ffn_g[0][None, :], peer_w_q=peer_w_q[0].astype(BF16),
        peer_keys=peer_sub_keys[0].reshape(2 * PEER_HEADS, N_KEYS, PEER_HALF).astype(BF16),
        peer_u=peer_u[0].astype(BF16), peer_vt=peer_v[0].astype(BF16).T,
        norm_final_g=norm_final_g[None, :])


def _trunk(x, conv_st, dconv_st, dn_st, wd, chunk):
    bsz, t_len, _ = x.shape
    n = bsz * t_len
    x2 = x.reshape(n, D_MODEL)
    proj = _in_proj(x2, wd["norm_mix_g"], wd["w_pack"])
    c_act, conv_new = _conv_branch(proj, conv_st, wd["conv_dw_w"], wd["conv_dw_b"], wd["conv_ln_g"],
                                   wd["conv_ln_b"], bsz, t_len)
    qkv, bg, dconv_new = _dn_prep(proj, dconv_st, wd["dn_conv_w"], wd["alog_pad"], wd["dt_pad"], bsz, t_len)
    beta = bg[:, :DN_HEADS]
    g = bg[:, DN_HEADS:2 * DN_HEADS]
    o, s_new = _gated_delta(qkv, beta, g, dn_st, bsz, t_len, chunk)
    x1 = _merge(c_act, o, proj, x2, wd["dn_norm_g"], wd["conv_pw2"], wd["dn_w_o"], wd["w_out"])
    xn2, s1, a, s2, b, tau = _peer_score(x1, wd["norm_ffn_g"], wd["peer_w_q"], wd["peer_keys"])
    peer_t = _peer_dense(xn2, wd["peer_u"], wd["peer_vt"], s1, a, s2, b, tau)
    y = _final(x1, peer_t, wd["norm_final_g"])
    return y.reshape(bsz, t_len, D_MODEL), conv_new[None], dconv_new[None], s_new[None]


def kernel(x_prompt, x_sample, state_conformer_conv, state_delta_conv, state_delta, norm_mix_g, w_in, conv_dw_w, conv_dw_b, conv_ln_g, conv_ln_b, conv_pw2, dn_conv_w, dn_a_log, dn_dt_bias, dn_norm_g, dn_w_o, w_out, norm_ffn_g, peer_w_q, peer_sub_keys, peer_u, peer_v, norm_final_g):
    wd = _prep_weights(norm_mix_g, w_in, conv_dw_w, conv_dw_b, conv_ln_g, conv_ln_b, conv_pw2, dn_conv_w,
                       dn_a_log, dn_dt_bias, dn_norm_g, dn_w_o, w_out, norm_ffn_g, peer_w_q, peer_sub_keys,
                       peer_u, peer_v, norm_final_g)
    bp = x_prompt.shape[0]
    zc = jnp.zeros((bp, CONV_WIDTH - 1, CONV_DIM), F32)
    zdc = jnp.zeros((bp, DN_CONV_WIDTH - 1, DN_CONV_CH), F32)
    zs = jnp.zeros((bp, DN_HEADS, DN_DK, DN_DV), F32)
    y_p, pc, pdc, ps = _trunk(x_prompt, zc, zdc, zs, wd, 64)
    y_s, sc, sdc, ss = _trunk(x_sample, state_conformer_conv[0], state_delta_conv[0], state_delta[0], wd,
                              x_sample.shape[1])
    return (y_p, y_s, pc, pdc, ps, sc, sdc, ss)
```

```python
import functools
import math

import jax
import jax.numpy as jnp
from jax import lax
from jax.experimental import pallas as pl
from jax.experimental.pallas import tpu as pltpu

F32 = jnp.float32
BF16 = jnp.bfloat16
HI = lax.Precision.HIGHEST

EPS = 1e-6
D_MODEL = 2048
CONV_DIM = 1024
CONV_WIDTH = 31
DN_HEADS = 8
DN_DK = 128
DN_DV = 128
DN_KEY = DN_HEADS * DN_DK
DN_VAL = DN_HEADS * DN_DV
DN_CONV_WIDTH = 4
DN_CONV_CH = 2 * DN_KEY + DN_VAL
PEER_HEADS = 8
PEER_HALF = 128
N_KEYS = 128
N_EXPERTS = N_KEYS * N_KEYS
PEER_TOPK = 16

LANES = 128
SUBLANES = 8
VMEM_LIMIT = 56 * 1024 * 1024

COL_QKV = 0
COL_GLU = COL_QKV + DN_CONV_CH
COL_Z = COL_GLU + 2 * CONV_DIM
COL_GATE = COL_Z + DN_VAL
COL_BA = COL_GATE + 2 * D_MODEL
PROJ_COLS = COL_BA + LANES


def _params(*sem, flags=None):
    return pltpu.CompilerParams(dimension_semantics=sem, vmem_limit_bytes=VMEM_LIMIT, flags=flags)


def _const_spec(shape):
    nd = len(shape)
    return pl.BlockSpec(shape, lambda *_: (0,) * nd, pipeline_mode=pl.Buffered(1))


def _sigmoid(x):
    return 0.5 * (1.0 + jnp.tanh(0.5 * x))


def _silu(x):
    return x * _sigmoid(x)


def _in_proj_kernel(x_ref, g_ref, w_ref, o_ref, xn_ref, *, rows):
    @pl.when(pl.program_id(1) == 0)
    def _():
        def body(i, carry):
            r = pl.ds(pl.multiple_of(i * rows, rows), rows)
            x = x_ref[r, :]
            y = x * lax.rsqrt(jnp.mean(x * x, axis=-1, keepdims=True) + EPS)
            xn_ref[r, :] = (y * g_ref[...]).astype(BF16)
            return carry
        lax.fori_loop(0, x_ref.shape[0] // rows, body, 0)

    o_ref[...] = jnp.dot(xn_ref[...], w_ref[...], preferred_element_type=F32)


def _in_proj(x2, g, w_pack):
    n = x2.shape[0]
    tm = min(n, 1024)
    tn = 1152
    return pl.pallas_call(
        functools.partial(_in_proj_kernel, rows=min(tm, 64)),
        grid=(n // tm, PROJ_COLS // tn),
        in_specs=[pl.BlockSpec((tm, D_MODEL), lambda i, j: (i, 0)),
                  pl.BlockSpec((1, D_MODEL), lambda i, j: (0, 0)),
                  pl.BlockSpec((D_MODEL, tn), lambda i, j: (0, j))],
        out_specs=pl.BlockSpec((tm, tn), lambda i, j: (i, j)),
        out_shape=jax.ShapeDtypeStruct((n, PROJ_COLS), F32),
        scratch_shapes=[pltpu.VMEM((tm, D_MODEL), BF16)],
        compiler_params=_params("arbitrary", "arbitrary"),
        name="in_proj",
    )(x2, g, w_pack)


def _conv_rows(xp_ref, w_ref, r0, rows, lanes, first, kw):
    acc = None
    k = 0
    while k < kw:
        off = first + k
        base = (off // SUBLANES) * SUBLANES
        nk = min(kw - k, base + SUBLANES - off)
        extra = SUBLANES if off + nk - 1 > base else 0
        win = xp_ref[pl.ds(pl.multiple_of(r0 + base, SUBLANES), rows + extra), lanes]
        for kk in range(k, k + nk):
            s = first + kk - base
            term = w_ref[kk:kk + 1, lanes] * win[s:s + rows, :]
            acc = term if acc is None else acc + term
        k += nk
    return acc


def _conv_rows_by_residue(xp_ref, w_ref, r0, rows, lanes, first, kw):
    y = None
    for b in range(SUBLANES):
        taps = [k for k in range(kw) if (first + k) % SUBLANES == b]
        if not taps:
            continue
        span = rows + (SUBLANES if b else 0)
        z = None
        for k in taps:
            start = pl.multiple_of(r0 + (first + k) // SUBLANES * SUBLANES, SUBLANES)
            term = w_ref[k:k + 1, lanes] * xp_ref[pl.ds(start, span), lanes]
            z = term if z is None else z + term
        z = z[b:b + rows, :]
        y = z if y is None else y + z
    return y


CONV_HDR = 32
CONV_LANE_CHUNK = 256


def _conv_branch_kernel(a_ref, b_ref, prev_ref, w_ref, bias_ref, lng_ref, lnb_ref,
                        c_ref, new_ref, xp_ref, y_ref, *, tt, rows):
    t = pl.program_id(1)
    p = CONV_WIDTH - 1

    xp_ref[0:CONV_HDR - p, :] = jnp.zeros((CONV_HDR - p, CONV_DIM), F32)

    @pl.when(t == 0)
    def _():
        xp_ref[CONV_HDR - p:CONV_HDR, :] = prev_ref[0]

    @pl.when(t > 0)
    def _():
        xp_ref[CONV_HDR - p:CONV_HDR, :] = xp_ref[CONV_HDR + tt - p:CONV_HDR + tt, :]

    xp_ref[CONV_HDR:CONV_HDR + tt, :] = a_ref[...] * _sigmoid(b_ref[...])

    def body(i, carry):
        r0 = pl.multiple_of(i * rows, rows)
        for lc in range(CONV_DIM // CONV_LANE_CHUNK):
            lanes = slice(lc * CONV_LANE_CHUNK, (lc + 1) * CONV_LANE_CHUNK)
            y_ref[:, lanes] = _conv_rows_by_residue(xp_ref, w_ref, r0, rows, lanes, CONV_HDR - p, CONV_WIDTH)
        y = y_ref[...] + bias_ref[...]
        mu = jnp.mean(y, axis=-1, keepdims=True)
        yc = y - mu
        var = jnp.mean(yc * yc, axis=-1, keepdims=True)
        z = yc * lax.rsqrt(var + EPS) * lng_ref[...] + lnb_ref[...]
        c_ref[pl.ds(r0, rows), :] = _silu(z).astype(BF16)
        return carry
    lax.fori_loop(0, tt // rows, body, 0)

    new_ref[0] = xp_ref[CONV_HDR + tt - p:CONV_HDR + tt, :]


def _conv_branch(proj, conv_prev, w, bias, lng, lnb, bsz, t_len):
    tt = min(t_len, 256)
    nt = t_len // tt
    p = CONV_WIDTH - 1
    blk_a = COL_GLU // CONV_DIM
    rows = min(tt, 32)
    return pl.pallas_call(
        functools.partial(_conv_branch_kernel, tt=tt, rows=rows),
        grid=(bsz, nt),
        in_specs=[pl.BlockSpec((tt, CONV_DIM), lambda b, t: (b * nt + t, blk_a)),
                  pl.BlockSpec((tt, CONV_DIM), lambda b, t: (b * nt + t, blk_a + 1)),
                  pl.BlockSpec((1, p, CONV_DIM), lambda b, t: (b, 0, 0)),
                  pl.BlockSpec((CONV_WIDTH, CONV_DIM), lambda b, t: (0, 0)),
                  pl.BlockSpec((1, CONV_DIM), lambda b, t: (0, 0)),
                  pl.BlockSpec((1, CONV_DIM), lambda b, t: (0, 0)),
                  pl.BlockSpec((1, CONV_DIM), lambda b, t: (0, 0))],
        out_specs=[pl.BlockSpec((tt, CONV_DIM), lambda b, t: (b * nt + t, 0)),
                   pl.BlockSpec((1, p, CONV_DIM), lambda b, t: (b, 0, 0))],
        out_shape=[jax.ShapeDtypeStruct((bsz * t_len, CONV_DIM), BF16),
                   jax.ShapeDtypeStruct((bsz, p, CONV_DIM), F32)],
        scratch_shapes=[pltpu.VMEM((CONV_HDR + tt, CONV_DIM), F32),
                        pltpu.VMEM((rows, CONV_DIM), F32)],
        compiler_params=_params("arbitrary", "arbitrary"),
        name="conv_branch",
    )(proj, proj, conv_prev, w, bias, lng, lnb)


DN_HDR = 8


def _dn_prep_kernel(x_ref, ba_ref, prev_ref, w_ref, alog_ref, dt_ref,
                    qkv_ref, bg_ref, new_ref, xp_ref, *, tt, rows):
    t = pl.program_id(1)
    p = DN_CONV_WIDTH - 1

    xp_ref[0:DN_HDR - p, :] = jnp.zeros((DN_HDR - p, DN_CONV_CH), F32)

    @pl.when(t == 0)
    def _():
        xp_ref[DN_HDR - p:DN_HDR, :] = prev_ref[0]

    @pl.when(t > 0)
    def _():
        xp_ref[DN_HDR - p:DN_HDR, :] = xp_ref[DN_HDR + tt - p:DN_HDR + tt, :]

    xp_ref[DN_HDR:DN_HDR + tt, :] = x_ref[...]

    def body(i, carry):
        r0 = pl.multiple_of(i * rows, rows)
        for hs in range(DN_CONV_CH // LANES):
            lanes = slice(hs * LANES, (hs + 1) * LANES)
            y = _silu(_conv_rows(xp_ref, w_ref, r0, rows, lanes, DN_HDR - p, DN_CONV_WIDTH))
            if hs < 2 * DN_HEADS:
                y = y * lax.rsqrt(jnp.sum(y * y, axis=-1, keepdims=True) + EPS)
                if hs < DN_HEADS:
                    y = y * (DN_DK ** -0.5)
            qkv_ref[pl.ds(r0, rows), lanes] = y
        return carry
    lax.fori_loop(0, tt // rows, body, 0)

    raw = ba_ref[...]
    sp_in = raw + dt_ref[...]
    softplus = jnp.maximum(sp_in, 0.0) + jnp.log(1.0 + jnp.exp(-jnp.abs(sp_in)))
    lane = lax.broadcasted_iota(jnp.int32, raw.shape, 1)
    bg_ref[...] = jnp.where(lane < DN_HEADS, _sigmoid(raw), -jnp.exp(alog_ref[...]) * softplus)

    new_ref[0] = xp_ref[DN_HDR + tt - p:DN_HDR + tt, :]


def _dn_prep(proj, dconv_prev, w, alog_pad, dt_pad, bsz, t_len):
    tt = min(t_len, 256)
    nt = t_len // tt
    p = DN_CONV_WIDTH - 1
    return pl.pallas_call(
        functools.partial(_dn_prep_kernel, tt=tt, rows=16),
        grid=(bsz, nt),
        in_specs=[pl.BlockSpec((tt, DN_CONV_CH), lambda b, t: (b * nt + t, COL_QKV // DN_CONV_CH)),
                  pl.BlockSpec((tt, LANES), lambda b, t: (b * nt + t, COL_BA // LANES)),
                  pl.BlockSpec((1, p, DN_CONV_CH), lambda b, t: (b, 0, 0)),
                  pl.BlockSpec((DN_CONV_WIDTH, DN_CONV_CH), lambda b, t: (0, 0)),
                  pl.BlockSpec((1, LANES), lambda b, t: (0, 0)),
                  pl.BlockSpec((1, LANES), lambda b, t: (0, 0))],
        out_specs=[pl.BlockSpec((tt, DN_CONV_CH), lambda b, t: (b * nt + t, 0)),
                   pl.BlockSpec((tt, LANES), lambda b, t: (b * nt + t, 0)),
                   pl.BlockSpec((1, p, DN_CONV_CH), lambda b, t: (b, 0, 0))],
        out_shape=[jax.ShapeDtypeStruct((bsz * t_len, DN_CONV_CH), F32),
                   jax.ShapeDtypeStruct((bsz * t_len, LANES), F32),
                   jax.ShapeDtypeStruct((bsz, p, DN_CONV_CH), F32)],
        scratch_shapes=[pltpu.VMEM((DN_HDR + tt, DN_CONV_CH), F32)],
        compiler_params=_params("arbitrary", "arbitrary"),
        name="dn_prep",
    )(proj, proj, dconv_prev, w, alog_pad, dt_pad)


def _dot_hi(a, b):
    return jnp.dot(a, b, precision=HI, preferred_element_type=F32)


_NN = (((1,), (0,)), ((), ()))
_NT = (((1,), (1,)), ((), ()))


def _split(a):
    hi = a.astype(BF16)
    lo = (a - hi.astype(F32)).astype(BF16)
    return hi, lo


def _mm3(a, b, dims=_NN, passes=3):
    a_hi, a_lo = a
    b_hi, b_lo = b
    out = lax.dot_general(a_hi, b_hi, dims, preferred_element_type=F32)
    if passes == 3:
        out = (out + lax.dot_general(a_hi, b_lo, dims, preferred_element_type=F32)
               + lax.dot_general(a_lo, b_hi, dims, preferred_element_type=F32))
    return out


PASSES_GRAM = 1
PASSES_INVERSE = 1
PASSES_STATE = 1


def _gated_delta_kernel(q_ref, k_ref, v_ref, bcol_ref, gcol_ref, grow_ref, s0_ref, o_ref, s_ref, *, chunk):
    c = pl.program_id(1)

    @pl.when(c == 0)
    def _():
        s_ref[...] = s0_ref[...]

    row = lax.broadcasted_iota(jnp.int32, (chunk, chunk), 0)
    col = lax.broadcasted_iota(jnp.int32, (chunk, chunk), 1)
    causal = row >= col
    strict = row > col
    lower = causal.astype(F32)
    upper = (row <= col).astype(F32)
    eye = (row == col).astype(F32)

    n_b = q_ref.shape[0]
    g_cum_col = [_dot_hi(lower, gcol_ref[b]) for b in range(n_b)]
    g_cum_row = [_dot_hi(grow_ref[b, 0], upper) for b in range(n_b)]
    n_sq = max(1, int(math.ceil(math.log2(chunk))) - 1)

    chains = [(b, hh) for b in range(n_b) for hh in range(DN_HEADS)]
    heads = range(len(chains))
    lanes = [slice(hh * DN_DK, (hh + 1) * DN_DK) for _, hh in chains]
    q = [q_ref[b, :, lanes[h]] for h, (b, _) in enumerate(chains)]
    k = [k_ref[b, :, lanes[h]] for h, (b, _) in enumerate(chains)]
    v = [v_ref[b, :, lanes[h]] for h, (b, _) in enumerate(chains)]
    beta = [bcol_ref[b, :, hh:hh + 1] for b, hh in chains]
    gc = [g_cum_col[b][:, hh:hh + 1] for b, hh in chains]
    gr = [g_cum_row[b][hh:hh + 1, :] for b, hh in chains]
    g_last = [gr[h][:, chunk - 1:chunk] for h in heads]
    k_sp = [_split(k[h]) for h in heads]
    q_sp = [_split(q[h]) for h in heads]
    s_old = [s_ref[b, hh] for b, hh in chains]
    s_sp = [_split(s_old[h]) for h in heads]
    decay = [jnp.where(causal, jnp.exp(jnp.where(causal, gc[h] - gr[h], 0.0)), 0.0) for h in heads]
    a_mat = [jnp.where(strict, beta[h] * _mm3(k_sp[h], k_sp[h], _NT, PASSES_GRAM) * decay[h], 0.0)
             for h in heads]
    qk = [_mm3(q_sp[h], k_sp[h], _NT, PASSES_GRAM) * decay[h] for h in heads]
    gam = [jnp.exp(gc[h]) for h in heads]
    qg_s = [_mm3(_split(q[h] * gam[h]), s_sp[h], _NN, PASSES_STATE) for h in heads]
    inv = [eye - a_mat[h] for h in heads]
    a_pow = [_split(a_mat[h]) for h in heads]
    for _ in range(n_sq):
        a_pow = [_split(_mm3(a_pow[h], a_pow[h], _NN, PASSES_INVERSE)) for h in heads]
        inv = [inv[h] + _mm3(_split(inv[h]), a_pow[h], _NN, PASSES_INVERSE) for h in heads]
    rhs = [jnp.concatenate([beta[h] * v[h], (beta[h] * gam[h]) * k[h]], axis=1) for h in heads]
    sol = [_mm3(_split(inv[h]), _split(rhs[h]), _NN, PASSES_INVERSE) for h in heads]
    delta = [sol[h][:, :DN_DV] - _mm3(_split(sol[h][:, DN_DV:]), s_sp[h], _NN, PASSES_STATE) for h in heads]
    d_sp = [_split(delta[h]) for h in heads]
    for h, (b, _) in enumerate(chains):
        o_ref[b, :, lanes[h]] = qg_s[h] + _mm3(_split(qk[h]), d_sp[h], _NN, PASSES_STATE)
    for h, (b, hh) in enumerate(chains):
        kd_t = (k[h] * jnp.exp(g_last[h] - gc[h])).T
        s_ref[b, hh] = jnp.exp(g_last[h]) * s_old[h] + _mm3(_split(kd_t), d_sp[h], _NN, PASSES_STATE)


DELTA_BATCH = 4


def _gated_delta(qkv, beta, g, s0, bsz, t_len, chunk):
    nc = t_len // chunk
    nb = DELTA_BATCH
    qkv3 = qkv.reshape(bsz, t_len, DN_CONV_CH)
    beta3 = beta.reshape(bsz, t_len, DN_HEADS)
    g3 = g.reshape(bsz, t_len, DN_HEADS)
    g_row = jnp.swapaxes(g.reshape(bsz, nc, chunk, DN_HEADS), 2, 3)
    o, s_new = pl.pallas_call(
        functools.partial(_gated_delta_kernel, chunk=chunk),
        grid=(bsz // nb, nc),
        in_specs=[pl.BlockSpec((nb, chunk, DN_KEY), lambda b, c: (b, c, 0)),
                  pl.BlockSpec((nb, chunk, DN_KEY), lambda b, c: (b, c, 1)),
                  pl.BlockSpec((nb, chunk, DN_VAL), lambda b, c: (b, c, 2)),
                  pl.BlockSpec((nb, chunk, DN_HEADS), lambda b, c: (b, c, 0)),
                  pl.BlockSpec((nb, chunk, DN_HEADS), lambda b, c: (b, c, 0)),
                  pl.BlockSpec((nb, 1, DN_HEADS, chunk), lambda b, c: (b, c, 0, 0)),
                  pl.BlockSpec((nb, DN_HEADS, DN_DK, DN_DV), lambda b, c: (b, 0, 0, 0))],
        out_specs=[pl.BlockSpec((nb, chunk, DN_VAL), lambda b, c: (b, c, 0)),
                   pl.BlockSpec((nb, DN_HEADS, DN_DK, DN_DV), lambda b, c: (b, 0, 0, 0))],
        out_shape=[jax.ShapeDtypeStruct((bsz, t_len, DN_VAL), F32),
                   jax.ShapeDtypeStruct((bsz, DN_HEADS, DN_DK, DN_DV), F32)],
        compiler_params=_params("arbitrary", "arbitrary"),
        name="gated_delta",
    )(qkv3, qkv3, qkv3, beta3, g3, g_row, s0)
    return o.reshape(bsz * t_len, DN_VAL), s_new


def _merge_kernel(c_ref, o_ref, z_ref, gc_ref, gd_ref, x_ref, ng_ref, pw2_ref, wo_ref, wout_ref,
                  x1_ref, og_ref):
    for h in range(DN_HEADS):
        lanes = slice(h * DN_DV, (h + 1) * DN_DV)
        oh = o_ref[:, lanes]
        oh = oh * lax.rsqrt(jnp.mean(oh * oh, axis=-1, keepdims=True) + EPS) * ng_ref[...]
        og_ref[:, lanes] = (oh * _silu(z_ref[:, lanes])).astype(BF16)
    y_dn = jnp.dot(og_ref[...], wo_ref[...], preferred_element_type=F32)
    y_conv = jnp.dot(c_ref[...], pw2_ref[...], preferred_element_type=F32)
    mix = _sigmoid(gc_ref[...]) * y_conv + _sigmoid(gd_ref[...]) * y_dn
    x1_ref[...] = x_ref[...] + jnp.dot(mix.astype(BF16), wout_ref[...], preferred_element_type=F32)


def _merge(c, o, proj, x2, ng, pw2, wo, wout):
    n = x2.shape[0]
    tm = min(n, 256)
    return pl.pallas_call(
        _merge_kernel,
        grid=(n // tm,),
        in_specs=[pl.BlockSpec((tm, CONV_DIM), lambda i: (i, 0)),
                  pl.BlockSpec((tm, DN_VAL), lambda i: (i, 0)),
                  pl.BlockSpec((tm, DN_VAL), lambda i: (i, COL_Z // DN_VAL)),
                  pl.BlockSpec((tm, D_MODEL), lambda i: (i, COL_GATE // D_MODEL)),
                  pl.BlockSpec((tm, D_MODEL), lambda i: (i, COL_GATE // D_MODEL + 1)),
                  pl.BlockSpec((tm, D_MODEL), lambda i: (i, 0)),
                  _const_spec((1, DN_DV)),
                  _const_spec((CONV_DIM, D_MODEL)),
                  _const_spec((DN_VAL, D_MODEL)),
                  _const_spec((D_MODEL, D_MODEL))],
        out_specs=pl.BlockSpec((tm, D_MODEL), lambda i: (i, 0)),
        out_shape=jax.ShapeDtypeStruct((n, D_MODEL), F32),
        scratch_shapes=[pltpu.VMEM((tm, DN_VAL), BF16)],
        compiler_params=_params("arbitrary"),
        name="merge",
    )(c, o, proj, proj, proj, x2, ng, pw2, wo, wout)


CAND_ROWS = 8 * 8 + 8 + 8


PEER_ET = 1024
PEER_KEY_BLOCK = SUBLANES
PEER_KEYS_PER_STEP = PEER_ET // N_KEYS
PEER_KEY_ROWS = N_KEYS // PEER_KEYS_PER_STEP * PEER_KEY_BLOCK


def _store_key_groups(dst_ref, h, lanes, x):
    for g in range(N_KEYS // PEER_KEYS_PER_STEP):
        dst_ref[h, g * PEER_KEY_BLOCK:g * PEER_KEY_BLOCK + PEER_KEYS_PER_STEP, lanes] = (
            x[g * PEER_KEYS_PER_STEP:(g + 1) * PEER_KEYS_PER_STEP, :])


def _sorting_network(n):
    size = 1
    while size < n:
        size *= 2
    pairs = []
    p = 1
    while p < size:
        k = p
        while k >= 1:
            for j in range(k % p, size - k, 2 * k):
                for i in range(min(k, size - j - k)):
                    if (i + j) // (2 * p) == (i + j + k) // (2 * p) and i + j + k < n:
                        pairs.append((i + j, i + j + k))
            k //= 2
        p *= 2
    return pairs


def _top_rows(xs, dsts, k):
    slabs = [[x[i:i + SUBLANES, :] for i in range(0, x.shape[0], SUBLANES)] for x in xs]
    for s in slabs:
        for i, j in _sorting_network(len(s)):
            s[i], s[j] = jnp.maximum(s[i], s[j]), jnp.minimum(s[i], s[j])
    for r in range(k):
        for c, (dst_ref, lanes) in enumerate(dsts):
            s = slabs[c]
            m = s[0]
            for shift in (4, 2, 1):
                m = jnp.maximum(m, pltpu.roll(m, shift, 0))
            dst_ref[r:r + 1, lanes] = m[0:1, :]
            popped = s[0] == m
            for d in range(min(len(s), k - 1 - r)):
                below = s[d + 1] if d + 1 < len(s) else -jnp.inf
                s[d] = jnp.where(popped, below, s[d])


PEER_HEADS_PER_ITER = 8


def _peer_score_kernel(x1_ref, g_ref, wq_ref, keys_ref, xn_ref, s1_ref, a_ref, s2_ref, b_ref, tau_ref,
                       q_scr, c1_all, c2_all, cand_all, top_all, *, tn):
    x = x1_ref[...]
    xn = x * lax.rsqrt(jnp.mean(x * x, axis=-1, keepdims=True) + EPS) * g_ref[...]
    xn_ref[...] = xn.T.astype(BF16)
    q = jnp.dot(xn.astype(BF16), wq_ref[...], preferred_element_type=F32)
    for hp in range(2 * PEER_HEADS):
        q_scr[hp] = q[:, hp * PEER_HALF:(hp + 1) * PEER_HALF].astype(BF16)

    def head(h, slot):
        c1_scr, c2_scr, cand_scr, top_scr = (c1_all.at[slot], c2_all.at[slot], cand_all.at[slot],
                                             top_all.at[slot])
        s1 = lax.dot_general(keys_ref[2 * h], q_scr[2 * h], (((1,), (1,)), ((), ())),
                             preferred_element_type=F32)
        s2 = lax.dot_general(keys_ref[2 * h + 1], q_scr[2 * h + 1], (((1,), (1,)), ((), ())),
                             preferred_element_type=F32)
        if PEER_KEYS_PER_STEP < PEER_KEY_BLOCK:
            s1_ref[h] = jnp.zeros(s1_ref.shape[1:], F32)
            a_ref[h] = jnp.zeros(a_ref.shape[1:], F32)
        _store_key_groups(s1_ref, h, slice(None), s1)
        s2_ref[h] = s2
        chunks = [slice(lc * LANES, (lc + 1) * LANES) for lc in range(tn // LANES)]
        for lanes in chunks:
            _top_rows([s1[:, lanes], s2[:, lanes]], [(c1_scr, lanes), (c2_scr, lanes)], PEER_TOPK)
        for lanes in chunks:
            c1 = c1_scr[:, lanes]
            c2 = c2_scr[:, lanes]
            for r1 in range(8):
                cand_scr[r1 * 8:(r1 + 1) * 8, lanes] = c1[r1:r1 + 1, :] + c2[0:8, :]
            cand_scr[64:72, lanes] = c1[8:16, :] + c2[0:1, :]
            cand_scr[72:80, lanes] = c1[0:1, :] + c2[8:16, :]
        _top_rows([cand_scr[:, lanes] for lanes in chunks], [(top_scr, lanes) for lanes in chunks], PEER_TOPK)
        for lanes in chunks:
            top = top_scr[:, lanes]
            m0 = top[0:1, :]
            zsum = jnp.sum(jnp.exp(top - m0), axis=0, keepdims=True)
            tau_ref[h, :, lanes] = top[PEER_TOPK - 1:PEER_TOPK, :]
            _store_key_groups(a_ref, h, lanes, jnp.exp(s1[:, lanes] - c1_scr[0:1, lanes]))
            b_ref[h, :, lanes] = jnp.exp(s2[:, lanes] - c2_scr[0:1, lanes]) / zsum

    def head_group(i, carry):
        for slot in range(PEER_HEADS_PER_ITER):
            head(i * PEER_HEADS_PER_ITER + slot, slot)
        return carry
    lax.fori_loop(0, PEER_HEADS // PEER_HEADS_PER_ITER, head_group, 0)


def _peer_score(x1, g, wq, keys):
    n = x1.shape[0]
    tn = min(n, 256)
    hk = (PEER_HEADS, N_KEYS, n)
    hk_spec = pl.BlockSpec((PEER_HEADS, N_KEYS, tn), lambda i: (0, 0, i))
    hi = (PEER_HEADS, PEER_KEY_ROWS, n)
    hi_spec = pl.BlockSpec((PEER_HEADS, PEER_KEY_ROWS, tn), lambda i: (0, 0, i))
    return pl.pallas_call(
        functools.partial(_peer_score_kernel, tn=tn),
        grid=(n // tn,),
        in_specs=[pl.BlockSpec((tn, D_MODEL), lambda i: (i, 0)),
                  _const_spec((1, D_MODEL)),
                  _const_spec((D_MODEL, 2 * PEER_HEADS * PEER_HALF)),
                  _const_spec((2 * PEER_HEADS, N_KEYS, PEER_HALF))],
        out_specs=[pl.BlockSpec((D_MODEL, tn), lambda i: (0, i)),
                   hi_spec, hi_spec, hk_spec, hk_spec,
                   pl.BlockSpec((PEER_HEADS, 1, tn), lambda i: (0, 0, i))],
        out_shape=[jax.ShapeDtypeStruct((D_MODEL, n), BF16),
                   jax.ShapeDtypeStruct(hi, F32), jax.ShapeDtypeStruct(hi, F32),
                   jax.ShapeDtypeStruct(hk, F32), jax.ShapeDtypeStruct(hk, F32),
                   jax.ShapeDtypeStruct((PEER_HEADS, 1, n), F32)],
        scratch_shapes=[pltpu.VMEM((2 * PEER_HEADS, tn, PEER_HALF), BF16),
                        pltpu.VMEM((PEER_HEADS_PER_ITER, PEER_TOPK, tn), F32),
                        pltpu.VMEM((PEER_HEADS_PER_ITER, PEER_TOPK, tn), F32),
                        pltpu.VMEM((PEER_HEADS_PER_ITER, CAND_ROWS, tn), F32),
                        pltpu.VMEM((PEER_HEADS_PER_ITER, PEER_TOPK, tn), F32)],
        compiler_params=_params("arbitrary"),
        name="peer_score",
    )(x1, g, wq, keys)


PEER_UP_PIECES = 2
PEER_DOWN_PIECES = 4
PEER_IG = 4
PEER_JB = 64

_GELU_C = math.sqrt(2.0 / math.pi)


def _gelu_tanh(x):
    return 0.5 * x * (1.0 + jnp.tanh(_GELU_C * (x + 0.044715 * (x * x * x))))


def _peer_dense_kernel(xn_ref, u_ref, vt_ref, s1_ref, a_ref, s2_ref, b_ref, tau_ref, o_ref,
                       act_scr, w_scr, hid_scr, *, tn, steps_per_tile):
    s = pl.program_id(0)
    prev = jnp.maximum(s - 1, 0)
    cur_slot = s % 2
    prev_slot = 1 - cur_slot

    @pl.when(s == 0)
    def _():
        act_scr[1] = jnp.zeros(act_scr.shape[1:], BF16)

    @pl.when(prev % steps_per_tile == 0)
    def _():
        o_ref[...] = jnp.zeros_like(o_ref)

    keys_per_step = PEER_ET // N_KEYS
    piece_w = min(tn, 2 * LANES)
    up_rows = PEER_ET // PEER_UP_PIECES
    down_rows = D_MODEL // PEER_DOWN_PIECES

    def up_proj(p, nq):
        rows, cols = slice(p * up_rows, (p + 1) * up_rows), slice(nq * piece_w, (nq + 1) * piece_w)
        hid_scr[rows, cols] = jnp.dot(u_ref[rows, :], xn_ref[:, cols], preferred_element_type=F32)

    def down_proj(p, nq):
        rows, cols = slice(p * down_rows, (p + 1) * down_rows), slice(nq * piece_w, (nq + 1) * piece_w)
        o_ref[rows, cols] += jnp.dot(vt_ref[rows, :], act_scr[prev_slot, :, cols],
                                     preferred_element_type=F32)

    def routing(ig, lc, jb):
        lanes = slice(lc * LANES, (lc + 1) * LANES)
        js = slice(jb * PEER_JB, (jb + 1) * PEER_JB)
        wgt = [None] * PEER_IG
        for h in range(PEER_HEADS):
            s2 = s2_ref[h, js, lanes]
            bb = b_ref[h, js, lanes]
            tau = tau_ref[h, :, lanes]
            for ii in range(PEER_IG):
                i_loc = ig * PEER_IG + ii
                sel = jnp.where(s1_ref[h, i_loc:i_loc + 1, lanes] + s2 >= tau, bb, 0.0)
                term = a_ref[h, i_loc:i_loc + 1, lanes] * sel
                wgt[ii] = term if wgt[ii] is None else wgt[ii] + term
        for ii in range(PEER_IG):
            r0 = (ig * PEER_IG + ii) * N_KEYS + jb * PEER_JB
            w_scr[r0:r0 + PEER_JB, lanes] = wgt[ii]

    def activate(p, nq):
        rows, cols = slice(p * up_rows, (p + 1) * up_rows), slice(nq * piece_w, (nq + 1) * piece_w)
        act_scr[cur_slot, rows, cols] = (_gelu_tanh(hid_scr[rows, cols]) * w_scr[rows, cols]).astype(BF16)

    n_q = tn // piece_w
    mxu_work = ([functools.partial(up_proj, p, nq) for p in range(PEER_UP_PIECES) for nq in range(n_q)]
                + [functools.partial(down_proj, p, nq) for p in range(PEER_DOWN_PIECES) for nq in range(n_q)])
    vpu_work = []
    groups_per_piece = keys_per_step // PEER_UP_PIECES // PEER_IG
    for p in range(PEER_UP_PIECES):
        for nq in range(n_q):
            for lc in range(nq * piece_w // LANES, (nq + 1) * piece_w // LANES):
                vpu_work += [functools.partial(routing, ig, lc, jb)
                             for ig in range(p * groups_per_piece, (p + 1) * groups_per_piece)
                             for jb in range(N_KEYS // PEER_JB)]
            vpu_work.append(functools.partial(activate, p, nq))
    emitted = 0
    for j, work in enumerate(vpu_work):
        while emitted * len(vpu_work) <= j * len(mxu_work) and emitted < len(mxu_work):
            mxu_work[emitted]()
            emitted += 1
        work()
    for work in mxu_work[emitted:]:
        work()


def _peer_dense(xn, u_bf, vt_bf, s1, a, s2, b, tau):
    n = xn.shape[1]
    tn = min(n, 512)
    spt = N_EXPERTS // PEER_ET
    total = (n // tn) * spt

    def cur(s):
        return jnp.minimum(s, total - 1)

    def prev(s):
        return jnp.maximum(s - 1, 0)

    hk_spec = pl.BlockSpec((PEER_HEADS, N_KEYS, tn), lambda s: (0, 0, cur(s) // spt))
    hi_spec = pl.BlockSpec((PEER_HEADS, PEER_KEY_BLOCK, tn), lambda s: (0, cur(s) % spt, cur(s) // spt))
    return pl.pallas_call(
        functools.partial(_peer_dense_kernel, tn=tn, steps_per_tile=spt),
        grid=(total + 1,),
        in_specs=[pl.BlockSpec((D_MODEL, tn), lambda s: (0, cur(s) // spt)),
                  pl.BlockSpec((PEER_ET, D_MODEL), lambda s: (cur(s) % spt, 0)),
                  pl.BlockSpec((D_MODEL, PEER_ET), lambda s: (0, prev(s) % spt)),
                  hi_spec, hi_spec, hk_spec, hk_spec,
                  pl.BlockSpec((PEER_HEADS, 1, tn), lambda s: (0, 0, cur(s) // spt))],
        out_specs=pl.BlockSpec((D_MODEL, tn), lambda s: (0, prev(s) // spt)),
        out_shape=jax.ShapeDtypeStruct((D_MODEL, n), F32),
        scratch_shapes=[pltpu.VMEM((2, PEER_ET, tn), BF16),
                        pltpu.VMEM((PEER_ET, tn), F32),
                        pltpu.VMEM((PEER_ET, tn), F32)],
        compiler_params=_params("arbitrary"),
        name="peer_dense",
    )(xn, u_bf, vt_bf, s1, a, s2, b, tau)


def _final_kernel(x1_ref, pt_ref, g_ref, y_ref):
    x = x1_ref[...] + pt_ref[...].T
    y_ref[...] = x * lax.rsqrt(jnp.mean(x * x, axis=-1, keepdims=True) + EPS) * g_ref[...]


def _final(x1, peer_t, g):
    n = x1.shape[0]
    tn = min(n, 256)
    return pl.pallas_call(
        _final_kernel,
        grid=(n // tn,),
        in_specs=[pl.BlockSpec((tn, D_MODEL), lambda i: (i, 0)),
                  pl.BlockSpec((D_MODEL, tn), lambda i: (0, i)),
                  pl.BlockSpec((1, D_MODEL), lambda i: (0, 0))],
        out_specs=pl.BlockSpec((tn, D_MODEL), lambda i: (i, 0)),
        out_shape=jax.ShapeDtypeStruct((n, D_MODEL), F32),
        compiler_params=_params("arbitrary"),
        name="final",
    )(x1, peer_t, g)


def _prep_weights(norm_mix_g, w_in, conv_dw_w, conv_dw_b, conv_ln_g, conv_ln_b, conv_pw2, dn_conv_w,
                  dn_a_log, dn_dt_bias, dn_norm_g, dn_w_o, w_out, norm_ffn_g, peer_w_q, peer_sub_keys,
                  peer_u, peer_v, norm_final_g):
    w = w_in[0].astype(BF16)
    glu_end = 2 * CONV_DIM
    qkv_end = glu_end + DN_CONV_CH
    ba_end = qkv_end + 2 * DN_HEADS
    z_end = ba_end + DN_VAL
    w_pack = jnp.concatenate(
        [w[:, glu_end:qkv_end], w[:, :glu_end], w[:, ba_end:z_end], w[:, z_end:], w[:, qkv_end:ba_end],
         jnp.zeros((D_MODEL, LANES - 2 * DN_HEADS), w.dtype)], axis=1).astype(BF16)
    pad_lo = jnp.zeros((DN_HEADS,), F32)
    pad_hi = jnp.zeros((LANES - 2 * DN_HEADS,), F32)
    return dict(
        norm_mix_g=norm_mix_g[0][None, :], w_pack=w_pack,
        conv_dw_w=conv_dw_w[0], conv_dw_b=conv_dw_b[0][None, :],
        conv_ln_g=conv_ln_g[0][None, :], conv_ln_b=conv_ln_b[0][None, :],
        conv_pw2=conv_pw2[0].astype(BF16), dn_conv_w=dn_conv_w[0],
        alog_pad=jnp.concatenate([pad_lo, dn_a_log[0], pad_hi])[None, :],
        dt_pad=jnp.concatenate([pad_lo, dn_dt_bias[0], pad_hi])[None, :],
        dn_norm_g=dn_norm_g[0][None, :], dn_w_o=dn_w_o[0].astype(BF16), w_out=w_out[0].astype(BF16),
        norm_ffn_g=norm_ffn_g[0][None, :], peer_w_q=peer_w_q[0].astype(BF16),
        peer_keys=peer_sub_keys[0].reshape(2 * PEER_HEADS, N_KEYS, PEER_HALF).astype(BF16),
        peer_u=peer_u[0].astype(BF16), peer_vt=peer_v[0].astype(BF16).T,
        norm_final_g=norm_final_g[None, :])


def _trunk(x, conv_st, dconv_st, dn_st, wd, chunk):
    bsz, t_len, _ = x.shape
    n = bsz * t_len
    x2 = x.reshape(n, D_MODEL)
    proj = _in_proj(x2, wd["norm_mix_g"], wd["w_pack"])
    c_act, conv_new = _conv_branch(proj, conv_st, wd["conv_dw_w"], wd["conv_dw_b"], wd["conv_ln_g"],
                                   wd["conv_ln_b"], bsz, t_len)
    qkv, bg, dconv_new = _dn_prep(proj, dconv_st, wd["dn_conv_w"], wd["alog_pad"], wd["dt_pad"], bsz, t_len)
    beta = bg[:, :DN_HEADS]
    g = bg[:, DN_HEADS:2 * DN_HEADS]
    o, s_new = _gated_delta(qkv, beta, g, dn_st, bsz, t_len, chunk)
    x1 = _merge(c_act, o, proj, x2, wd["dn_norm_g"], wd["conv_pw2"], wd["dn_w_o"], wd["w_out"])
    xn2, s1, a, s2, b, tau = _peer_score(x1, wd["norm_ffn_g"], wd["peer_w_q"], wd["peer_keys"])
    peer_t = _peer_dense(xn2, wd["peer_u"], wd["peer_vt"], s1, a, s2, b, tau)
    y = _final(x1, peer_t, wd["norm_final_g"])
    return y.reshape(bsz, t_len, D_MODEL), conv_new[None], dconv_new[None], s_new[None]


def kernel(x_prompt, x_sample, state_conformer_conv, state_delta_conv, state_delta, norm_mix_g, w_in, conv_dw_w, conv_dw_b, conv_ln_g, conv_ln_b, conv_pw2, dn_conv_w, dn_a_log, dn_dt_bias, dn_norm_g, dn_w_o, w_out, norm_ffn_g, peer_w_q, peer_sub_keys, peer_u, peer_v, norm_final_g):
    wd = _prep_weights(norm_mix_g, w_in, conv_dw_w, conv_dw_b, conv_ln_g, conv_ln_b, conv_pw2, dn_conv_w,
                       dn_a_log, dn_dt_bias, dn_norm_g, dn_w_o, w_out, norm_ffn_g, peer_w_q, peer_sub_keys,
                       peer_u, peer_v, norm_final_g)
    bp = x_prompt.shape[0]
    zc = jnp.zeros((bp, CONV_WIDTH - 1, CONV_DIM), F32)
    zdc = jnp.zeros((bp, DN_CONV_WIDTH - 1, DN_CONV_CH), F32)
    zs = jnp.zeros((bp, DN_HEADS, DN_DK, DN_DV), F32)
    y_p, pc, pdc, ps = _trunk(x_prompt, zc, zdc, zs, wd, 64)
    y_s, sc, sdc, ss = _trunk(x_sample, state_conformer_conv[0], state_delta_conv[0], state_delta[0], wd,
                              x_sample.shape[1])
    return (y_p, y_s, pc, pdc, ps, sc, sdc, ss)
```
